```python
import math
import jax, jax.numpy as jnp
from jax import lax
import numpy as np

D_MODEL = 1024
BATCH = 16
SEQ = 2048
DEPTH = 1

HEAD_DIM = 64
SWA_HEADS = 8
SWA_KV_HEADS = 2
SWA_WINDOW = 128
SWA_BLOCK = 128
MOBA_HEADS = 8
MOBA_BLOCK = 256
MOBA_TOPK = 3
MOBA_Q_CHUNK = 16
N_EXPERTS = 64
TOP_K = 8
N_GROUPS = 8
TOPK_GROUPS = 4
EXPERT_FF = 256
SHARED_FF = 256
ROUTED_SCALE = 2.5
DISPATCH_BLOCK = 128
DEEPNORM_ALPHA = (2.0 * DEPTH) ** 0.25
DEEPNORM_BETA = (8.0 * DEPTH) ** -0.25
LN_EPS = 1e-5
NEG_INF = -1e30
SWA_Q_W = SWA_HEADS * HEAD_DIM
SWA_KV_W = SWA_KV_HEADS * HEAD_DIM
MOBA_W = MOBA_HEADS * HEAD_DIM
IN_COLS = SWA_Q_W + 2 * SWA_KV_W + 3 * MOBA_W + 2 * D_MODEL

kernel_name = "hybrid_swa_moba_gated_moe_deepnorm"

F32 = jnp.float32


def _alibi_slopes():
    n = SWA_HEADS + MOBA_HEADS
    s = 2.0 ** (-8.0 * np.arange(1, n + 1) / n)
    return jnp.asarray(s[:SWA_HEADS], F32), jnp.asarray(s[SWA_HEADS:], F32)


def _layernorm(x, g, b):
    xf = x.astype(F32)
    mu = xf.mean(-1, keepdims=True)
    var = jnp.mean(jnp.square(xf - mu), -1, keepdims=True)
    return ((xf - mu) * lax.rsqrt(var + LN_EPS) * g.astype(F32) + b.astype(F32)).astype(x.dtype)


def _swa_attention(q, k, v, sinks, slopes):
    B, S = q.shape[:2]
    G = SWA_HEADS // SWA_KV_HEADS
    nq = S // SWA_BLOCK
    qb = q.reshape(B, nq, SWA_BLOCK, SWA_KV_HEADS, G, HEAD_DIM)

    def band(t):
        tb = t.reshape(B, nq, SWA_BLOCK, SWA_KV_HEADS, HEAD_DIM)
        prev = jnp.pad(tb[:, :-1], ((0, 0), (1, 0), (0, 0), (0, 0), (0, 0)))
        return jnp.concatenate([prev, tb], axis=2)

    kk, vv = band(k), band(v)
    s = jnp.einsum('bnqhgd,bnkhd->bnhgqk', qb, kk, preferred_element_type=F32) * (HEAD_DIM ** -0.5)
    i = jnp.arange(SWA_BLOCK)[:, None]
    j = jnp.arange(2 * SWA_BLOCK)[None, :]
    dist = i + SWA_BLOCK - j
    kpos = jnp.arange(nq)[:, None, None] * SWA_BLOCK + j - SWA_BLOCK
    valid = (dist >= 0) & (dist < SWA_WINDOW) & (kpos >= 0)
    slope = slopes.reshape(SWA_KV_HEADS, G)[:, :, None, None]
    s = jnp.where(valid[None, :, None, None], s - slope * dist.astype(F32), NEG_INF)
    sink = sinks.astype(F32).reshape(SWA_KV_HEADS, G)[:, :, None, None]
    m = jnp.maximum(s.max(-1, keepdims=True), sink)
    e = jnp.exp(s - m)
    p = e / (e.sum(-1, keepdims=True) + jnp.exp(sink - m))
    o = jnp.einsum('bnhgqk,bnkhd->bnqhgd', p.astype(v.dtype), vv)
    return o.reshape(B, S, SWA_Q_W)


def _moba_attention(q, k, v, slopes):
    B, S, H, Dh = q.shape
    nb = -(-S // MOBA_BLOCK)
    s_pad = nb * MOBA_BLOCK
    pad = ((0, 0), (0, s_pad - S), (0, 0), (0, 0))
    k_blocks = jnp.pad(k, pad).reshape(B, nb, MOBA_BLOCK, H, Dh)
    k_mean = k_blocks.astype(F32).mean(2)
    kbh = k_blocks.transpose(0, 3, 1, 2, 4)
    vbh = jnp.pad(v, pad).reshape(B, nb, MOBA_BLOCK, H, Dh).transpose(0, 3, 1, 2, 4)
    tk = min(MOBA_TOPK, nb)
    scale = Dh ** -0.5
    b_ix = jnp.arange(B)[:, None, None]
    h_ix = jnp.arange(H)[None, :, None]
    j = jnp.arange(MOBA_BLOCK)
    QC = MOBA_Q_CHUNK

    def chunk(c):
        t0 = c * QC
        qc = lax.dynamic_slice_in_dim(q, t0, QC, axis=1).transpose(0, 2, 1, 3)
        t = t0 + jnp.arange(QC)
        own = t0 // MOBA_BLOCK
        gate = jnp.einsum('bhqd,bnhd->bhqn', qc.astype(F32), k_mean)
        gate = jnp.where(jnp.arange(nb) < own, gate, NEG_INF)
        _, sel = lax.top_k(gate, tk)
        sel_valid = jnp.arange(tk) < own
        flat = sel.reshape(B, H, QC * tk)
        kg = kbh[b_ix, h_ix, flat].reshape(B, H, QC, tk, MOBA_BLOCK, Dh)
        vg = vbh[b_ix, h_ix, flat].reshape(B, H, QC, tk, MOBA_BLOCK, Dh)
        s_sel = jnp.einsum('bhqd,bhqnkd->bhqnk', qc, kg, preferred_element_type=F32) * scale
        dist_sel = t[:, None, None] - (sel[..., None] * MOBA_BLOCK + j)
        s_sel = jnp.where(sel_valid[:, None], s_sel - slopes[:, None, None, None] * dist_sel.astype(F32), NEG_INF)
        k_own = lax.dynamic_slice_in_dim(kbh, own, 1, axis=2)[:, :, 0]
        v_own = lax.dynamic_slice_in_dim(vbh, own, 1, axis=2)[:, :, 0]
        s_own = jnp.einsum('bhqd,bhkd->bhqk', qc, k_own, preferred_element_type=F32) * scale
        dist_own = t[:, None] - (own * MOBA_BLOCK + j)[None, :]
        s_own = jnp.where(dist_own >= 0, s_own - slopes[:, None, None] * dist_own.astype(F32), NEG_INF)
        s_all = jnp.concatenate([s_sel.reshape(B, H, QC, tk * MOBA_BLOCK), s_own], axis=-1)
        p = jax.nn.softmax(s_all, axis=-1).astype(v.dtype)
        p_sel = p[..., :tk * MOBA_BLOCK].reshape(B, H, QC, tk, MOBA_BLOCK)
        p_own = p[..., tk * MOBA_BLOCK:]
        o = jnp.einsum('bhqnk,bhqnkd->bhqd', p_sel, vg) + jnp.einsum('bhqk,bhkd->bhqd', p_own, v_own)
        return o.transpose(0, 2, 1, 3)

    out = lax.map(chunk, jnp.arange(S // QC))
    return out.transpose(1, 0, 2, 3, 4).reshape(B, S, H * Dh)


def _swiglu(x, wg, wu, wd):
    return (jax.nn.silu(x @ wg) * (x @ wu)) @ wd


def _grouped_experts(xf, top_idx, top_w, w_gate, w_up, w_down):
    T = xf.shape[0]
    n_pairs = T * TOP_K
    e_flat = top_idx.reshape(-1)
    tok_flat = jnp.repeat(jnp.arange(T, dtype=jnp.int32), TOP_K)
    w_flat = top_w.reshape(-1)
    order = jnp.argsort(e_flat)
    e_s, tok_s, w_s = e_flat[order], tok_flat[order], w_flat[order]
    counts = jnp.bincount(e_flat, length=N_EXPERTS)
    padded = (counts + DISPATCH_BLOCK - 1) // DISPATCH_BLOCK * DISPATCH_BLOCK
    raw_start = jnp.cumsum(counts) - counts
    pad_end = jnp.cumsum(padded)
    pad_start = pad_end - padded
    dest = pad_start[e_s] + (jnp.arange(n_pairs) - raw_start[e_s])
    n_pad = n_pairs + N_EXPERTS * DISPATCH_BLOCK
    n_blk = n_pad // DISPATCH_BLOCK
    buf_tok = jnp.zeros((n_pad,), jnp.int32).at[dest].set(tok_s)
    buf_w = jnp.zeros((n_pad,), xf.dtype).at[dest].set(w_s)
    blk_e = jnp.minimum(jnp.searchsorted(pad_end, jnp.arange(n_blk) * DISPATCH_BLOCK, side='right'), N_EXPERTS - 1)

    def step(acc, inp):
        tok, wt, e = inp
        xb = xf[tok]
        y = _swiglu(xb, w_gate[e], w_up[e], w_down[e]) * wt[:, None]
        return acc.at[tok].add(y), None

    out, _ = lax.scan(step, jnp.zeros_like(xf),
                      (buf_tok.reshape(n_blk, DISPATCH_BLOCK), buf_w.reshape(n_blk, DISPATCH_BLOCK), blk_e))
    return out


def _moe(xf, w_router, router_bias, w_exp_gate, w_exp_up, w_exp_down, w_sh_gate, w_sh_up, w_sh_down):
    T = xf.shape[0]
    scores = jax.nn.sigmoid(jnp.einsum('td,de->te', xf, w_router, preferred_element_type=F32))
    biased = scores + router_bias.astype(F32)
    grp = biased.reshape(T, N_GROUPS, N_EXPERTS // N_GROUPS)
    grp_score = lax.top_k(grp, 2)[0].sum(-1)
    _, grp_idx = lax.top_k(grp_score, TOPK_GROUPS)
    grp_mask = jax.nn.one_hot(grp_idx, N_GROUPS, dtype=F32).sum(-2) > 0
    exp_mask = jnp.repeat(grp_mask, N_EXPERTS // N_GROUPS, axis=-1)
    _, top_idx = lax.top_k(jnp.where(exp_mask, biased, NEG_INF), TOP_K)
    top_w = jnp.take_along_axis(scores, top_idx, axis=-1)
    top_w = top_w / top_w.sum(-1, keepdims=True) * ROUTED_SCALE
    routed = _grouped_experts(xf, top_idx, top_w.astype(xf.dtype), w_exp_gate, w_exp_up, w_exp_down)
    return routed + _swiglu(xf, w_sh_gate, w_sh_up, w_sh_down)


def setup_inputs(seed: int = 0) -> dict:
    key = jax.random.key(seed)
    ks = jax.random.split(key, 18)
    D, L, E, F = D_MODEL, DEPTH, N_EXPERTS, EXPERT_FF
    nrm = jax.random.normal
    col_scale = np.ones((IN_COLS,), np.float32)
    va0 = SWA_Q_W + SWA_KV_W
    col_scale[va0:va0 + SWA_KV_W] = DEEPNORM_BETA
    vb0 = SWA_Q_W + 2 * SWA_KV_W + 2 * MOBA_W
    col_scale[vb0:vb0 + MOBA_W] = DEEPNORM_BETA
    return {
        "x": nrm(ks[0], (BATCH, SEQ, D), F32),
        "w_in": nrm(ks[1], (L, D, IN_COLS), F32) * (D ** -0.5) * jnp.asarray(col_scale),
        "swa_sinks": nrm(ks[2], (L, SWA_HEADS), F32) * 0.5,
        "w_branch_swa": nrm(ks[3], (L, SWA_Q_W, D), F32) * (SWA_Q_W ** -0.5) * DEEPNORM_BETA,
        "w_branch_moba": nrm(ks[4], (L, MOBA_W, D), F32) * (MOBA_W ** -0.5) * DEEPNORM_BETA,
        "w_out": nrm(ks[5], (L, D, D), F32) * (D ** -0.5) * DEEPNORM_BETA,
        "ln1_g": 1.0 + 0.05 * nrm(ks[6], (L, D), F32),
        "ln1_b": 0.02 * nrm(ks[7], (L, D), F32),
        "w_router": nrm(ks[8], (L, D, E), F32) * (D ** -0.5),
        "router_bias": 0.01 * nrm(ks[9], (L, E), F32),
        "w_exp_gate": nrm(ks[10], (L, E, D, F), F32) * (D ** -0.5) * DEEPNORM_BETA,
        "w_exp_up": nrm(ks[11], (L, E, D, F), F32) * (D ** -0.5) * DEEPNORM_BETA,
        "w_exp_down": nrm(ks[12], (L, E, F, D), F32) * (F ** -0.5) * DEEPNORM_BETA,
        "w_sh_gate": nrm(ks[13], (L, D, SHARED_FF), F32) * (D ** -0.5) * DEEPNORM_BETA,
        "w_sh_up": nrm(ks[14], (L, D, SHARED_FF), F32) * (D ** -0.5) * DEEPNORM_BETA,
        "w_sh_down": nrm(ks[15], (L, SHARED_FF, D), F32) * (SHARED_FF ** -0.5) * DEEPNORM_BETA,
        "ln2_g": 1.0 + 0.05 * nrm(ks[16], (L, D), F32),
        "ln2_b": 0.02 * nrm(ks[17], (L, D), F32),
    }


def reference(x, w_in, swa_sinks, w_branch_swa, w_branch_moba, w_out, ln1_g, ln1_b, w_router, router_bias,
              w_exp_gate, w_exp_up, w_exp_down, w_sh_gate, w_sh_up, w_sh_down, ln2_g, ln2_b):
    B, S, D = x.shape
    slopes_swa, slopes_moba = _alibi_slopes()
    sizes = [SWA_Q_W, SWA_KV_W, SWA_KV_W, MOBA_W, MOBA_W, MOBA_W, D_MODEL, D_MODEL]
    offsets = []
    acc = 0
    for sz in sizes[:-1]:
        acc += sz
        offsets.append(acc)
    h = x
    for l in range(DEPTH):
        proj = h @ w_in[l]
        qa, ka, va, qb, kb, vb, ga, gb = jnp.split(proj, offsets, axis=-1)
        att_a = _swa_attention(qa.reshape(B, S, SWA_HEADS, HEAD_DIM), ka.reshape(B, S, SWA_KV_HEADS, HEAD_DIM),
                               va.reshape(B, S, SWA_KV_HEADS, HEAD_DIM), swa_sinks[l], slopes_swa)
        att_b = _moba_attention(qb.reshape(B, S, MOBA_HEADS, HEAD_DIM), kb.reshape(B, S, MOBA_HEADS, HEAD_DIM),
                                vb.reshape(B, S, MOBA_HEADS, HEAD_DIM), slopes_moba)
        y_a = att_a @ w_branch_swa[l]
        y_b = att_b @ w_branch_moba[l]
        mix = (jax.nn.sigmoid(ga) * y_a + jax.nn.sigmoid(gb) * y_b) @ w_out[l]
        h = _layernorm(DEEPNORM_ALPHA * h + mix, ln1_g[l], ln1_b[l])
        ffn = _moe(h.reshape(B * S, D), w_router[l], router_bias[l], w_exp_gate[l], w_exp_up[l], w_exp_down[l],
                   w_sh_gate[l], w_sh_up[l], w_sh_down[l]).reshape(B, S, D)
        h = _layernorm(DEEPNORM_ALPHA * h + ffn, ln2_g[l], ln2_b[l])
    return h
```

```python
import functools

import numpy as np
import jax
import jax.numpy as jnp
from jax import lax
from jax.experimental import pallas as pl
from jax.experimental.pallas import tpu as pltpu

F32 = jnp.float32
BF16 = jnp.bfloat16

HEAD_DIM = 64
SWA_HEADS = 8
SWA_KV_HEADS = 2
SWA_BLOCK = 128
MOBA_HEADS = 8
MOBA_BLOCK = 256
MOBA_TOPK = 3
N_EXPERTS = 64
TOP_K = 8
N_GROUPS = 8
TOPK_GROUPS = 4
ROUTED_SCALE = 2.5
LN_EPS = 1e-5
NEG_INF = -1e30
REMOVED = -3e38

SWA_Q_W = SWA_HEADS * HEAD_DIM
SWA_KV_W = SWA_KV_HEADS * HEAD_DIM
MOBA_W = MOBA_HEADS * HEAD_DIM
QKV_COLS = SWA_Q_W + 2 * SWA_KV_W + 3 * MOBA_W

_N_SOFTMAX_HEADS = SWA_HEADS + MOBA_HEADS
_SLOPES = np.asarray(2.0 ** (-8.0 * np.arange(1, _N_SOFTMAX_HEADS + 1) / _N_SOFTMAX_HEADS), np.float32)
SWA_SLOPES = [float(s) for s in _SLOPES[:SWA_HEADS]]
MOBA_SLOPES = [float(s) for s in _SLOPES[SWA_HEADS:]]

VMEM_LIMIT_BYTES = 56 * 1024 * 1024


def _dot(a, b):
    return jnp.dot(a, b, preferred_element_type=F32)


def _dot_nt(a, b, precision=None):
    return lax.dot_general(a, b, (((1,), (1,)), ((), ())), preferred_element_type=F32, precision=precision)


def _sigmoid(x):
    return 1.0 / (1.0 + jnp.exp(-x))


def _layernorm(z, g, b):
    mu = jnp.mean(z, axis=-1, keepdims=True)
    zc = z - mu
    var = jnp.mean(zc * zc, axis=-1, keepdims=True)
    return zc * lax.rsqrt(var + LN_EPS) * g + b


def _qkv_kernel(x_ref, w_ref, qa_ref, ka_ref, va_ref, qb_ref, kb_ref, vb_ref, km_ref, *, tm):
    xb = x_ref[...].astype(BF16)
    scale = HEAD_DIM ** -0.5

    def proj(lo, hi):
        return _dot(xb, w_ref[:, lo:hi])

    c = 0
    qa_ref[...] = (proj(c, c + SWA_Q_W) * scale).astype(BF16)
    c += SWA_Q_W
    ka_ref[...] = proj(c, c + SWA_KV_W).astype(BF16)
    c += SWA_KV_W
    va_ref[...] = proj(c, c + SWA_KV_W).astype(BF16)
    c += SWA_KV_W
    qb_ref[...] = (proj(c, c + MOBA_W) * scale).astype(BF16)
    c += MOBA_W
    kb = proj(c, c + MOBA_W)
    kb_ref[...] = kb.astype(BF16)
    for i in range(tm // MOBA_BLOCK):
        blk = kb[i * MOBA_BLOCK:(i + 1) * MOBA_BLOCK, :]
        km_ref[0, i:i + 1, :] = jnp.sum(blk, axis=0, keepdims=True) * (1.0 / MOBA_BLOCK)
    c += MOBA_W
    vb_ref[...] = proj(c, c + MOBA_W).astype(BF16)


def _qkv_proj(xf, w_qkv, tm=512):
    T, D = xf.shape
    nt = T // tm
    row = lambda w: pl.BlockSpec((tm, w), lambda i: (i, 0))
    outs = pl.pallas_call(
        functools.partial(_qkv_kernel, tm=tm),
        grid=(nt,),
        in_specs=[row(D), pl.BlockSpec((D, QKV_COLS), lambda i: (0, 0))],
        out_specs=[row(SWA_Q_W), row(SWA_KV_W), row(SWA_KV_W), row(MOBA_W), row(MOBA_W), row(MOBA_W),
                   pl.BlockSpec((1, tm // MOBA_BLOCK, MOBA_W), lambda i: (i, 0, 0))],
        out_shape=[jax.ShapeDtypeStruct((T, SWA_Q_W), BF16), jax.ShapeDtypeStruct((T, SWA_KV_W), BF16),
                   jax.ShapeDtypeStruct((T, SWA_KV_W), BF16), jax.ShapeDtypeStruct((T, MOBA_W), BF16),
                   jax.ShapeDtypeStruct((T, MOBA_W), BF16), jax.ShapeDtypeStruct((T, MOBA_W), BF16),
                   jax.ShapeDtypeStruct((nt, tm // MOBA_BLOCK, MOBA_W), F32)],
        compiler_params=pltpu.CompilerParams(dimension_semantics=("parallel",), vmem_limit_bytes=VMEM_LIMIT_BYTES),
        name="qkv_proj",
    )(xf, w_qkv)
    return outs


def _swa_kernel(sink_ref, q_ref, kp_ref, ko_ref, vp_ref, vo_ref, o_ref):
    n = pl.program_id(1)
    blk = SWA_BLOCK
    k = jnp.concatenate([kp_ref[...], ko_ref[...]], axis=0)
    v = jnp.concatenate([vp_ref[...], vo_ref[...]], axis=0)
    i = lax.broadcasted_iota(jnp.int32, (blk, 2 * blk), 0)
    j = lax.broadcasted_iota(jnp.int32, (blk, 2 * blk), 1)
    dist = i + blk - j
    valid = (dist >= 0) & (dist < blk) & ((j >= blk) | (n > 0))
    distf = dist.astype(F32)
    group = SWA_HEADS // SWA_KV_HEADS
    outs = []
    for h in range(SWA_HEADS):
        kv = h // group
        qh = q_ref[:, h * HEAD_DIM:(h + 1) * HEAD_DIM]
        kh = k[:, kv * HEAD_DIM:(kv + 1) * HEAD_DIM]
        vh = v[:, kv * HEAD_DIM:(kv + 1) * HEAD_DIM]
        s = _dot_nt(qh, kh)
        s = jnp.where(valid, s - SWA_SLOPES[h] * distf, NEG_INF)
        sink = sink_ref[h]
        m = jnp.maximum(jnp.max(s, axis=-1, keepdims=True), sink)
        e = jnp.exp(s - m)
        denom = jnp.sum(e, axis=-1, keepdims=True) + jnp.exp(sink - m)
        outs.append(_dot(e.astype(BF16), vh) / denom)
    o_ref[...] = jnp.concatenate(outs, axis=-1).astype(BF16)


def _swa_attention(qa, ka, va, sinks, B, S):
    nq = S // SWA_BLOCK
    own = lambda b, n: (b * nq + n, 0)
    prev = lambda b, n: (b * nq + jnp.maximum(n - 1, 0), 0)
    kv_spec = lambda im: pl.BlockSpec((SWA_BLOCK, SWA_KV_W), im)
    return pl.pallas_call(
        _swa_kernel,
        grid=(B, nq),
        in_specs=[pl.BlockSpec(memory_space=pltpu.SMEM),
                  pl.BlockSpec((SWA_BLOCK, SWA_Q_W), own),
                  kv_spec(prev), kv_spec(own), kv_spec(prev), kv_spec(own)],
        out_specs=pl.BlockSpec((SWA_BLOCK, SWA_Q_W), own),
        out_shape=jax.ShapeDtypeStruct((B * S, SWA_Q_W), BF16),
        compiler_params=pltpu.CompilerParams(dimension_semantics=("parallel", "parallel")),
        name="swa_attention",
    )(sinks, qa, ka, ka, va, va)


def _moba_kernel(q_ref, k_ref, v_ref, km_ref, o_ref, m_sc, l_sc, acc_sc, *, nb):
    hp = pl.program_id(1)
    qi = pl.program_id(2)
    blk = MOBA_BLOCK
    i = lax.broadcasted_iota(jnp.int32, (blk, blk), 0)
    j = lax.broadcasted_iota(jnp.int32, (blk, blk), 1)
    causal = i >= j
    d0 = (i - j).astype(F32)
    n_iota = lax.broadcasted_iota(jnp.int32, (nb, blk), 0)
    outs = []
    for hh in range(2):
        lanes = slice(hh * HEAD_DIM, (hh + 1) * HEAD_DIM)
        qh = q_ref[:, lanes]
        slope = jnp.where(hp == 0, MOBA_SLOPES[hh], 0.0)
        for p in range(1, MOBA_HEADS // 2):
            slope = jnp.where(hp == p, MOBA_SLOPES[2 * p + hh], slope)

        g_t = _dot_nt(km_ref[0, :, lanes], qh.astype(F32), precision=lax.Precision.HIGHEST)
        rank = jnp.zeros((nb, blk), jnp.int32)
        for mblk in range(nb):
            gm = g_t[mblk:mblk + 1, :]
            beats = (gm > g_t) | ((gm == g_t) & (mblk < n_iota))
            rank = rank + jnp.where(beats & (mblk < qi), 1, 0)
        sel = (n_iota < qi) & (rank < MOBA_TOPK)
        selb = jnp.where(sel, 0.0, NEG_INF)
        selb = jnp.concatenate([selb, jnp.zeros((128 - nb, blk), F32)], axis=0)
        selb_t = selb.T

        a0 = slope * d0

        def scores(n_start):
            kn = k_ref[pl.ds(n_start, blk), lanes]
            return _dot_nt(qh, kn)

        s = jnp.where(causal, scores(pl.multiple_of(qi * blk, blk)) - a0, NEG_INF)
        m0 = jnp.max(s, axis=-1, keepdims=True)
        p0 = jnp.exp(s - m0)
        m_sc[...] = m0
        l_sc[...] = jnp.sum(p0, axis=-1, keepdims=True)
        acc_sc[...] = _dot(p0.astype(BF16), v_ref[pl.ds(pl.multiple_of(qi * blk, blk), blk), lanes])

        for n in range(nb - 1):
            @pl.when(n < qi)
            def _(n=n):
                off = slope * ((qi - n) * blk).astype(F32)
                s = scores(n * blk) - a0 - off + selb_t[:, n:n + 1]
                m_old = m_sc[...]
                m_new = jnp.maximum(m_old, jnp.max(s, axis=-1, keepdims=True))
                alpha = jnp.exp(m_old - m_new)
                p = jnp.exp(s - m_new)
                m_sc[...] = m_new
                l_sc[...] = alpha * l_sc[...] + jnp.sum(p, axis=-1, keepdims=True)
                acc_sc[...] = alpha * acc_sc[...] + _dot(p.astype(BF16), v_ref[n * blk:(n + 1) * blk, lanes])

        outs.append(acc_sc[...] / l_sc[...])
    o_ref[...] = jnp.concatenate(outs, axis=-1).astype(BF16)


def _moba_attention(qb, kb, vb, kmean, B, S):
    nb = S // MOBA_BLOCK
    blk = MOBA_BLOCK
    hpairs = MOBA_HEADS // 2
    return pl.pallas_call(
        functools.partial(_moba_kernel, nb=nb),
        grid=(B, hpairs, nb),
        in_specs=[pl.BlockSpec((blk, 2 * HEAD_DIM), lambda b, h, q: (b * nb + q, h)),
                  pl.BlockSpec((S, 2 * HEAD_DIM), lambda b, h, q: (b, h)),
                  pl.BlockSpec((S, 2 * HEAD_DIM), lambda b, h, q: (b, h)),
                  pl.BlockSpec((1, nb, 2 * HEAD_DIM), lambda b, h, q: (b, 0, h))],
        out_specs=pl.BlockSpec((blk, 2 * HEAD_DIM), lambda b, h, q: (b * nb + q, h)),
        out_shape=jax.ShapeDtypeStruct((B * S, MOBA_W), BF16),
        scratch_shapes=[pltpu.VMEM((blk, 1), F32), pltpu.VMEM((blk, 1), F32), pltpu.VMEM((blk, HEAD_DIM), F32)],
        compiler_params=pltpu.CompilerParams(dimension_semantics=("parallel", "parallel", "arbitrary")),
        name="moba_attention",
    )(qb, kb, vb, kmean)


def _route(scores_t, biased, tm):
    gsz = N_EXPERTS // N_GROUPS
    sub = lax.broadcasted_iota(jnp.int32, (gsz, tm), 0)
    grp = [biased[g * gsz:(g + 1) * gsz, :] for g in range(N_GROUPS)]
    gscore = []
    for v in grp:
        m1 = jnp.max(v, axis=0, keepdims=True)
        first = jnp.min(jnp.where(v == m1, sub, gsz), axis=0, keepdims=True)
        m2 = jnp.max(jnp.where(sub == first, REMOVED, v), axis=0, keepdims=True)
        gscore.append(m1 + m2)
    masked = []
    for g in range(N_GROUPS):
        rank = jnp.zeros((1, tm), jnp.int32)
        for g2 in range(N_GROUPS):
            if g2 == g:
                continue
            beats = (gscore[g2] >= gscore[g]) if g2 < g else (gscore[g2] > gscore[g])
            rank = rank + jnp.where(beats, 1, 0)
        masked.append(jnp.where(rank < TOPK_GROUPS, grp[g], NEG_INF))
    v = jnp.concatenate(masked, axis=0)
    e_iota = lax.broadcasted_iota(jnp.int32, (N_EXPERTS, tm), 0)
    wsel = jnp.zeros((N_EXPERTS, tm), F32)
    for _ in range(TOP_K):
        m = jnp.max(v, axis=0, keepdims=True)
        first = jnp.min(jnp.where(v == m, e_iota, N_EXPERTS), axis=0, keepdims=True)
        pick = e_iota == first
        wsel = jnp.where(pick, scores_t, wsel)
        v = jnp.where(pick, REMOVED, v)
    return wsel / jnp.sum(wsel, axis=0, keepdims=True) * ROUTED_SCALE


def _mix_kernel(x_ref, aa_ref, ab_ref, wg_ref, wa_ref, wb_ref, wo_ref, g_ref, b_ref, wr_ref, rb_ref,
                h_ref, hb_ref, w_ref, *, alpha, tm):
    x = x_ref[...]
    xb = x.astype(BF16)
    d = x.shape[-1]
    ga = _dot(xb, wg_ref[:, :d])
    gb = _dot(xb, wg_ref[:, d:])
    y = _sigmoid(ga) * _dot(aa_ref[...], wa_ref[...]) + _sigmoid(gb) * _dot(ab_ref[...], wb_ref[...])
    mix = _dot(y.astype(BF16), wo_ref[...])
    h = _layernorm(alpha * x + mix, g_ref[...], b_ref[...])
    h_ref[...] = h
    hb_ref[...] = h.astype(BF16)
    logits_t = _dot_nt(wr_ref[...], h, precision=lax.Precision.HIGHEST)
    scores_t = _sigmoid(logits_t)
    w_t = _route(scores_t, scores_t + rb_ref[...], tm)
    w_pad = jnp.concatenate([w_t, jnp.zeros((128 - N_EXPERTS, tm), F32)], axis=0)
    w_ref[...] = w_pad.T


def _mix_ln_route(xf, att_a, att_b, w_gates, w_a, w_b, w_o, ln_g, ln_b, w_router_t, router_bias, alpha, tm=256):
    T, D = xf.shape
    row = lambda w: pl.BlockSpec((tm, w), lambda i: (i, 0))
    full = lambda a: pl.BlockSpec(a.shape, lambda i: (0,) * a.ndim)
    return pl.pallas_call(
        functools.partial(_mix_kernel, alpha=alpha, tm=tm),
        grid=(T // tm,),
        in_specs=[row(D), row(SWA_Q_W), row(MOBA_W), full(w_gates), full(w_a), full(w_b), full(w_o),
                  full(ln_g), full(ln_b), full(w_router_t), full(router_bias)],
        out_specs=[row(D), row(D), row(128)],
        out_shape=[jax.ShapeDtypeStruct((T, D), F32), jax.ShapeDtypeStruct((T, D), BF16),
                   jax.ShapeDtypeStruct((T, 128), F32)],
        compiler_params=pltpu.CompilerParams(dimension_semantics=("parallel",), vmem_limit_bytes=VMEM_LIMIT_BYTES),
        name="mix_ln_route",
    )(xf, att_a, att_b, w_gates, w_a, w_b, w_o, ln_g, ln_b, w_router_t, router_bias)


def _moe_kernel(h_ref, hb_ref, w_ref, eg_ref, eu_ref, ed_ref, sg_ref, su_ref, sd_ref, g_ref, b_ref,
                o_ref, acc_sc, *, alpha):
    e = pl.program_id(1)
    hb = hb_ref[...]

    def swiglu(wg, wu, scale=None):
        a = _dot(hb, wg)
        hid = a * _sigmoid(a) * _dot(hb, wu)
        if scale is not None:
            hid = hid * scale
        return hid.astype(BF16)

    @pl.when(e == 0)
    def _():
        acc_sc[...] = _dot(swiglu(sg_ref[...], su_ref[...]), sd_ref[...])

    lane = lax.broadcasted_iota(jnp.int32, w_ref.shape, 1)
    wcol = jnp.sum(jnp.where(lane == e, w_ref[...], 0.0), axis=-1, keepdims=True)
    acc_sc[...] += _dot(swiglu(eg_ref[0], eu_ref[0], wcol), ed_ref[0])

    @pl.when(e == pl.num_programs(1) - 1)
    def _():
        o_ref[...] = _layernorm(alpha * h_ref[...] + acc_sc[...], g_ref[...], b_ref[...])


def _moe(h, hb, w_dense, eg, eu, ed, sg, su, sd, ln_g, ln_b, alpha, tm=1024):
    T, D = h.shape
    E, _, F = eg.shape
    row = lambda w: pl.BlockSpec((tm, w), lambda i, e: (i, 0))
    full = lambda a: pl.BlockSpec(a.shape, lambda i, e: (0,) * a.ndim)
    return pl.pallas_call(
        functools.partial(_moe_kernel, alpha=alpha),
        grid=(T // tm, E),
        in_specs=[row(D), row(D), row(128),
                  pl.BlockSpec((1, D, F), lambda i, e: (e, 0, 0)),
                  pl.BlockSpec((1, D, F), lambda i, e: (e, 0, 0)),
                  pl.BlockSpec((1, F, D), lambda i, e: (e, 0, 0)),
                  full(sg), full(su), full(sd), full(ln_g), full(ln_b)],
        out_specs=row(D),
        out_shape=jax.ShapeDtypeStruct((T, D), F32),
        scratch_shapes=[pltpu.VMEM((tm, D), F32)],
        compiler_params=pltpu.CompilerParams(dimension_semantics=("parallel", "arbitrary"),
                                             vmem_limit_bytes=VMEM_LIMIT_BYTES),
        name="moe_experts",
    )(h, hb, w_dense, eg, eu, ed, sg, su, sd, ln_g, ln_b)


def kernel(x, w_in, swa_sinks, w_branch_swa, w_branch_moba, w_out, ln1_g, ln1_b, w_router, router_bias,
           w_exp_gate, w_exp_up, w_exp_down, w_sh_gate, w_sh_up, w_sh_down, ln2_g, ln2_b):
    B, S, D = x.shape
    depth = w_in.shape[0]
    alpha = (2.0 * depth) ** 0.25
    h = x.reshape(B * S, D)
    for l in range(depth):
        w_in_b = w_in[l].astype(BF16)
        qa, ka, va, qb, kb, vb, kmean = _qkv_proj(h, w_in_b[:, :QKV_COLS])
        att_a = _swa_attention(qa, ka, va, swa_sinks[l], B, S)
        att_b = _moba_attention(qb, kb, vb, kmean.reshape(B, S // MOBA_BLOCK, MOBA_W), B, S)
        h1, h1b, w_dense = _mix_ln_route(
            h, att_a, att_b, w_in_b[:, QKV_COLS:], w_branch_swa[l].astype(BF16), w_branch_moba[l].astype(BF16),
            w_out[l].astype(BF16), ln1_g[l].reshape(1, D), ln1_b[l].reshape(1, D),
            w_router[l].T, router_bias[l].reshape(N_EXPERTS, 1), alpha)
        h = _moe(h1, h1b, w_dense, w_exp_gate[l].astype(BF16), w_exp_up[l].astype(BF16),
                 w_exp_down[l].astype(BF16), w_sh_gate[l].astype(BF16), w_sh_up[l].astype(BF16),
                 w_sh_down[l].astype(BF16), ln2_g[l].reshape(1, D), ln2_b[l].reshape(1, D), alpha)
    return h.reshape(B, S, D)
```

```python
import functools

import numpy as np
import jax
import jax.numpy as jnp
from jax import lax
from jax.experimental import pallas as pl
from jax.experimental.pallas import tpu as pltpu

F32 = jnp.float32
BF16 = jnp.bfloat16

HEAD_DIM = 64
SWA_HEADS = 8
SWA_KV_HEADS = 2
SWA_BLOCK = 128
MOBA_HEADS = 8
MOBA_BLOCK = 256
MOBA_TOPK = 3
N_EXPERTS = 64
TOP_K = 8
N_GROUPS = 8
TOPK_GROUPS = 4
ROUTED_SCALE = 2.5
LN_EPS = 1e-5
NEG_INF = -1e30
REMOVED = -3e38

SWA_Q_W = SWA_HEADS * HEAD_DIM
SWA_KV_W = SWA_KV_HEADS * HEAD_DIM
MOBA_W = MOBA_HEADS * HEAD_DIM
QKV_COLS = SWA_Q_W + 2 * SWA_KV_W + 3 * MOBA_W

_N_SOFTMAX_HEADS = SWA_HEADS + MOBA_HEADS
_SLOPES = np.asarray(2.0 ** (-8.0 * np.arange(1, _N_SOFTMAX_HEADS + 1) / _N_SOFTMAX_HEADS), np.float32)
SWA_SLOPES = [float(s) for s in _SLOPES[:SWA_HEADS]]
MOBA_SLOPES = [float(s) for s in _SLOPES[SWA_HEADS:]]

VMEM_LIMIT_BYTES = 56 * 1024 * 1024


def _dot(a, b):
    return jnp.dot(a, b, preferred_element_type=F32)


def _dot_nt(a, b, precision=None):
    return lax.dot_general(a, b, (((1,), (1,)), ((), ())), preferred_element_type=F32, precision=precision)


def _sigmoid(x):
    return 1.0 / (1.0 + jnp.exp(-x))


def _layernorm(z, g, b):
    mu = jnp.mean(z, axis=-1, keepdims=True)
    zc = z - mu
    var = jnp.mean(zc * zc, axis=-1, keepdims=True)
    return zc * lax.rsqrt(var + LN_EPS) * g + b


def _qkv_kernel(x_ref, w_ref, qa_ref, ka_ref, va_ref, qb_ref, kb_ref, vb_ref, km_ref, *, tm):
    xb = x_ref[...].astype(BF16)
    scale = HEAD_DIM ** -0.5

    def proj(lo, hi):
        return _dot(xb, w_ref[:, lo:hi])

    c = 0
    qa_ref[...] = (proj(c, c + SWA_Q_W) * scale).astype(BF16)
    c += SWA_Q_W
    ka_ref[...] = proj(c, c + SWA_KV_W).astype(BF16)
    c += SWA_KV_W
    va_ref[...] = proj(c, c + SWA_KV_W).astype(BF16)
    c += SWA_KV_W
    qb_ref[...] = (proj(c, c + MOBA_W) * scale).astype(BF16)
    c += MOBA_W
    kb = proj(c, c + MOBA_W)
    kb_ref[...] = kb.astype(BF16)
    for i in range(tm // MOBA_BLOCK):
        blk = kb[i * MOBA_BLOCK:(i + 1) * MOBA_BLOCK, :]
        km_ref[0, i:i + 1, :] = jnp.sum(blk, axis=0, keepdims=True) * (1.0 / MOBA_BLOCK)
    c += MOBA_W
    vb_ref[...] = proj(c, c + MOBA_W).astype(BF16)


def _qkv_proj(xf, w_qkv, tm=512):
    T, D = xf.shape
    nt = T // tm
    row = lambda w: pl.BlockSpec((tm, w), lambda i: (i, 0))
    outs = pl.pallas_call(
        functools.partial(_qkv_kernel, tm=tm),
        grid=(nt,),
        in_specs=[row(D), pl.BlockSpec((D, QKV_COLS), lambda i: (0, 0))],
        out_specs=[row(SWA_Q_W), row(SWA_KV_W), row(SWA_KV_W), row(MOBA_W), row(MOBA_W), row(MOBA_W),
                   pl.BlockSpec((1, tm // MOBA_BLOCK, MOBA_W), lambda i: (i, 0, 0))],
        out_shape=[jax.ShapeDtypeStruct((T, SWA_Q_W), BF16), jax.ShapeDtypeStruct((T, SWA_KV_W), BF16),
                   jax.ShapeDtypeStruct((T, SWA_KV_W), BF16), jax.ShapeDtypeStruct((T, MOBA_W), BF16),
                   jax.ShapeDtypeStruct((T, MOBA_W), BF16), jax.ShapeDtypeStruct((T, MOBA_W), BF16),
                   jax.ShapeDtypeStruct((nt, tm // MOBA_BLOCK, MOBA_W), F32)],
        compiler_params=pltpu.CompilerParams(dimension_semantics=("parallel",), vmem_limit_bytes=VMEM_LIMIT_BYTES),
        name="qkv_proj",
    )(xf, w_qkv)
    return outs


def _swa_kernel(sink_ref, q_ref, kp_ref, ko_ref, vp_ref, vo_ref, o_ref):
    n = pl.program_id(1)
    blk = SWA_BLOCK
    k = jnp.concatenate([kp_ref[...], ko_ref[...]], axis=0)
    v = jnp.concatenate([vp_ref[...], vo_ref[...]], axis=0)
    i = lax.broadcasted_iota(jnp.int32, (blk, 2 * blk), 0)
    j = lax.broadcasted_iota(jnp.int32, (blk, 2 * blk), 1)
    dist = i + blk - j
    valid = (dist >= 0) & (dist < blk) & ((j >= blk) | (n > 0))
    distf = dist.astype(F32)
    group = SWA_HEADS // SWA_KV_HEADS
    outs = []
    for h in range(SWA_HEADS):
        kv = h // group
        qh = q_ref[:, h * HEAD_DIM:(h + 1) * HEAD_DIM]
        kh = k[:, kv * HEAD_DIM:(kv + 1) * HEAD_DIM]
        vh = v[:, kv * HEAD_DIM:(kv + 1) * HEAD_DIM]
        s = _dot_nt(qh, kh)
        s = jnp.where(valid, s - SWA_SLOPES[h] * distf, NEG_INF)
        sink = sink_ref[h]
        m = jnp.maximum(jnp.max(s, axis=-1, keepdims=True), sink)
        e = jnp.exp(s - m)
        denom = jnp.sum(e, axis=-1, keepdims=True) + jnp.exp(sink - m)
        outs.append(_dot(e.astype(BF16), vh) / denom)
    o_ref[...] = jnp.concatenate(outs, axis=-1).astype(BF16)


def _swa_attention(qa, ka, va, sinks, B, S):
    nq = S // SWA_BLOCK
    own = lambda b, n: (b * nq + n, 0)
    prev = lambda b, n: (b * nq + jnp.maximum(n - 1, 0), 0)
    kv_spec = lambda im: pl.BlockSpec((SWA_BLOCK, SWA_KV_W), im)
    return pl.pallas_call(
        _swa_kernel,
        grid=(B, nq),
        in_specs=[pl.BlockSpec(memory_space=pltpu.SMEM),
                  pl.BlockSpec((SWA_BLOCK, SWA_Q_W), own),
                  kv_spec(prev), kv_spec(own), kv_spec(prev), kv_spec(own)],
        out_specs=pl.BlockSpec((SWA_BLOCK, SWA_Q_W), own),
        out_shape=jax.ShapeDtypeStruct((B * S, SWA_Q_W), BF16),
        compiler_params=pltpu.CompilerParams(dimension_semantics=("parallel", "parallel")),
        name="swa_attention",
    )(sinks, qa, ka, ka, va, va)


MOBA_ROWS = 256


def _moba_bias_table(S, nb):
    pos = np.arange(S)
    onehot = np.zeros((S, MOBA_HEADS, HEAD_DIM), np.float32)
    onehot[pos, :, pos // MOBA_BLOCK] = 1.0
    tab = jnp.asarray(onehot)
    rem = jnp.asarray(np.asarray(MOBA_SLOPES, np.float32)[None, :] * pos[:, None].astype(np.float32))
    for t in range(3):
        part = rem.astype(BF16).astype(F32)
        tab = tab.at[:, :, nb + t].set(part)
        rem = rem - part
    return tab.reshape(S, MOBA_HEADS * HEAD_DIM).astype(BF16)


def _moba_kernel(q_ref, k_ref, v_ref, km_ref, tab_ref, o_ref, kaug_sc, s_sc, *, nb):
    qi = pl.program_id(2)
    blk = MOBA_BLOCK
    hd = HEAD_DIM
    rows_per = MOBA_ROWS

    @pl.when(qi == 0)
    def _():
        for hh in range(2):
            kaug_sc[:, 2 * hh * hd:(2 * hh + 1) * hd] = k_ref[:, hh * hd:(hh + 1) * hd]
            kaug_sc[:, (2 * hh + 1) * hd:(2 * hh + 2) * hd] = tab_ref[:, hh * hd:(hh + 1) * hd]

    ri = lax.broadcasted_iota(jnp.int32, (rows_per, blk), 0)
    cj = lax.broadcasted_iota(jnp.int32, (rows_per, blk), 1)
    n_iota = lax.broadcasted_iota(jnp.int32, (nb, blk), 0)
    lane64 = lax.broadcasted_iota(jnp.int32, (1, hd), 1)
    ones_row = jnp.where((lane64 >= nb) & (lane64 < nb + 3), 1.0, 0.0)
    out_lane = lax.broadcasted_iota(jnp.int32, (rows_per, 2 * hd), 1)

    def tile(c):
        q_augs = []
        for hh in range(2):
            lanes = slice(hh * hd, (hh + 1) * hd)
            qh = q_ref[:, lanes]
            g_t = _dot_nt(km_ref[0, :, lanes], qh.astype(F32), precision=lax.Precision.HIGHEST)
            rank = jnp.zeros((nb, blk), jnp.int32)
            for mblk in range(c):
                gm = g_t[mblk:mblk + 1, :]
                beats = (gm > g_t) | ((gm == g_t) & (mblk < n_iota))
                rank = rank + jnp.where(beats, 1, 0)
            keep = ((n_iota < c) & (rank < MOBA_TOPK)) | (n_iota == c)
            selb = jnp.where(keep, 0.0, NEG_INF)
            selb_t = jnp.concatenate([selb, jnp.zeros((128 - nb, blk), F32)], axis=0).T
            q_bias = (selb_t[:, :hd] + ones_row).astype(BF16)
            q_augs.append(jnp.concatenate([qh, q_bias], axis=1))
        res = []
        for hh in range(2):
            q_aug = q_augs[hh]
            per_chunk = []
            for rc in range(blk // rows_per):
                qa = q_aug[rc * rows_per:(rc + 1) * rows_per]
                mx = None
                for n in range(c + 1):
                    s = _dot_nt(qa, kaug_sc[n * blk:(n + 1) * blk, 2 * hh * hd:(2 * hh + 2) * hd])
                    if n == c:
                        s = jnp.where(ri + rc * rows_per >= cj, s, NEG_INF)
                    s_sc[hh, rc, n] = s
                    t = jnp.maximum(s[:, :128], s[:, 128:])
                    mx = t if mx is None else jnp.maximum(mx, t)
                m = jnp.max(mx, axis=-1, keepdims=True)
                lsum = None
                acc = None
                for n in range(c + 1):
                    p = jnp.exp(s_sc[hh, rc, n] - m)
                    t = p[:, :128] + p[:, 128:]
                    lsum = t if lsum is None else lsum + t
                    pv = _dot(p.astype(BF16), v_ref[n * blk:(n + 1) * blk, :])
                    acc = pv if acc is None else acc + pv
                per_chunk.append(acc / jnp.sum(lsum, axis=-1, keepdims=True))
            res.append(per_chunk)
        for rc in range(blk // rows_per):
            o = jnp.where(out_lane < hd, res[0][rc], res[1][rc])
            o_ref[rc * rows_per:(rc + 1) * rows_per, :] = o.astype(BF16)

    for c in range(nb):
        pl.when(qi == c)(functools.partial(tile, c))


def _moba_attention(qb, kb, vb, kmean, B, S):
    nb = S // MOBA_BLOCK
    blk = MOBA_BLOCK
    hpairs = MOBA_HEADS // 2
    pair = 2 * HEAD_DIM
    tab = _moba_bias_table(S, nb)
    return pl.pallas_call(
        functools.partial(_moba_kernel, nb=nb),
        grid=(B, hpairs, nb),
        in_specs=[pl.BlockSpec((blk, pair), lambda b, h, q: (b * nb + q, h)),
                  pl.BlockSpec((S, pair), lambda b, h, q: (b, h)),
                  pl.BlockSpec((S, pair), lambda b, h, q: (b, h)),
                  pl.BlockSpec((1, nb, pair), lambda b, h, q: (b, 0, h)),
                  pl.BlockSpec((S, pair), lambda b, h, q: (0, h))],
        out_specs=pl.BlockSpec((blk, pair), lambda b, h, q: (b * nb + q, h)),
        out_shape=jax.ShapeDtypeStruct((B * S, MOBA_W), BF16),
        scratch_shapes=[pltpu.VMEM((S, 2 * pair), BF16),
                        pltpu.VMEM((2, blk // MOBA_ROWS, nb, MOBA_ROWS, blk), F32)],
        compiler_params=pltpu.CompilerParams(dimension_semantics=("parallel", "parallel", "arbitrary"),
                                             vmem_limit_bytes=VMEM_LIMIT_BYTES),
        name="moba_attention",
    )(qb, kb, vb, kmean, tab)


def _route(scores_t, biased, tm):
    gsz = N_EXPERTS // N_GROUPS
    sub = lax.broadcasted_iota(jnp.int32, (gsz, tm), 0)
    grp = [biased[g * gsz:(g + 1) * gsz, :] for g in range(N_GROUPS)]
    gscore = []
    for v in grp:
        m1 = jnp.max(v, axis=0, keepdims=True)
        first = jnp.min(jnp.where(v == m1, sub, gsz), axis=0, keepdims=True)
        m2 = jnp.max(jnp.where(sub == first, REMOVED, v), axis=0, keepdims=True)
        gscore.append(m1 + m2)
    masked = []
    for g in range(N_GROUPS):
        rank = jnp.zeros((1, tm), jnp.int32)
        for g2 in range(N_GROUPS):
            if g2 == g:
                continue
            beats = (gscore[g2] >= gscore[g]) if g2 < g else (gscore[g2] > gscore[g])
            rank = rank + jnp.where(beats, 1, 0)
        masked.append(jnp.where(rank < TOPK_GROUPS, grp[g], NEG_INF))
    v = jnp.concatenate(masked, axis=0)
    e_iota = lax.broadcasted_iota(jnp.int32, (N_EXPERTS, tm), 0)
    wsel = jnp.zeros((N_EXPERTS, tm), F32)
    for _ in range(TOP_K):
        m = jnp.max(v, axis=0, keepdims=True)
        first = jnp.min(jnp.where(v == m, e_iota, N_EXPERTS), axis=0, keepdims=True)
        pick = e_iota == first
        wsel = jnp.where(pick, scores_t, wsel)
        v = jnp.where(pick, REMOVED, v)
    return wsel / jnp.sum(wsel, axis=0, keepdims=True) * ROUTED_SCALE


def _mix_kernel(x_ref, aa_ref, ab_ref, wg_ref, wa_ref, wb_ref, wo_ref, g_ref, b_ref, wr_ref, rb_ref,
                h_ref, hb_ref, w_ref, *, alpha, tm):
    x = x_ref[...]
    xb = x.astype(BF16)
    d = x.shape[-1]
    ga = _dot(xb, wg_ref[:, :d])
    gb = _dot(xb, wg_ref[:, d:])
    y = _sigmoid(ga) * _dot(aa_ref[...], wa_ref[...]) + _sigmoid(gb) * _dot(ab_ref[...], wb_ref[...])
    mix = _dot(y.astype(BF16), wo_ref[...])
    h = _layernorm(alpha * x + mix, g_ref[...], b_ref[...])
    h_ref[...] = h
    hb_ref[...] = h.astype(BF16)
    logits_t = _dot_nt(wr_ref[...], h, precision=lax.Precision.HIGHEST)
    scores_t = _sigmoid(logits_t)
    w_t = _route(scores_t, scores_t + rb_ref[...], tm)
    w_pad = jnp.concatenate([w_t, jnp.zeros((128 - N_EXPERTS, tm), F32)], axis=0)
    w_ref[...] = w_pad.T


def _mix_ln_route(xf, att_a, att_b, w_gates, w_a, w_b, w_o, ln_g, ln_b, w_router_t, router_bias, alpha, tm=256):
    T, D = xf.shape
    row = lambda w: pl.BlockSpec((tm, w), lambda i: (i, 0))
    full = lambda a: pl.BlockSpec(a.shape, lambda i: (0,) * a.ndim)
    return pl.pallas_call(
        functools.partial(_mix_kernel, alpha=alpha, tm=tm),
        grid=(T // tm,),
        in_specs=[row(D), row(SWA_Q_W), row(MOBA_W), full(w_gates), full(w_a), full(w_b), full(w_o),
                  full(ln_g), full(ln_b), full(w_router_t), full(router_bias)],
        out_specs=[row(D), row(D), row(128)],
        out_shape=[jax.ShapeDtypeStruct((T, D), F32), jax.ShapeDtypeStruct((T, D), BF16),
                   jax.ShapeDtypeStruct((T, 128), F32)],
        compiler_params=pltpu.CompilerParams(dimension_semantics=("parallel",), vmem_limit_bytes=VMEM_LIMIT_BYTES),
        name="mix_ln_route",
    )(xf, att_a, att_b, w_gates, w_a, w_b, w_o, ln_g, ln_b, w_router_t, router_bias)


def _moe_kernel(h_ref, hb_ref, w_ref, eg_ref, eu_ref, ed_ref, sg_ref, su_ref, sd_ref, g_ref, b_ref,
                o_ref, acc_sc, *, alpha):
    e = pl.program_id(1)
    hb = hb_ref[...]

    def swiglu(wg, wu, scale=None):
        a = _dot(hb, wg)
        hid = a * _sigmoid(a) * _dot(hb, wu)
        if scale is not None:
            hid = hid * scale
        return hid.astype(BF16)

    @pl.when(e == 0)
    def _():
        acc_sc[...] = _dot(swiglu(sg_ref[...], su_ref[...]), sd_ref[...])

    lane = lax.broadcasted_iota(jnp.int32, w_ref.shape, 1)
    wcol = jnp.sum(jnp.where(lane == e, w_ref[...], 0.0), axis=-1, keepdims=True)
    acc_sc[...] += _dot(swiglu(eg_ref[0], eu_ref[0], wcol), ed_ref[0])

    @pl.when(e == pl.num_programs(1) - 1)
    def _():
        o_ref[...] = _layernorm(alpha * h_ref[...] + acc_sc[...], g_ref[...], b_ref[...])


def _moe(h, hb, w_dense, eg, eu, ed, sg, su, sd, ln_g, ln_b, alpha, tm=1024):
    T, D = h.shape
    E, _, F = eg.shape
    row = lambda w: pl.BlockSpec((tm, w), lambda i, e: (i, 0))
    full = lambda a: pl.BlockSpec(a.shape, lambda i, e: (0,) * a.ndim)
    return pl.pallas_call(
        functools.partial(_moe_kernel, alpha=alpha),
        grid=(T // tm, E),
        in_specs=[row(D), row(D), row(128),
                  pl.BlockSpec((1, D, F), lambda i, e: (e, 0, 0)),
                  pl.BlockSpec((1, D, F), lambda i, e: (e, 0, 0)),
                  pl.BlockSpec((1, F, D), lambda i, e: (e, 0, 0)),
                  full(sg), full(su), full(sd), full(ln_g), full(ln_b)],
        out_specs=row(D),
        out_shape=jax.ShapeDtypeStruct((T, D), F32),
        scratch_shapes=[pltpu.VMEM((tm, D), F32)],
        compiler_params=pltpu.CompilerParams(dimension_semantics=("parallel", "arbitrary"),
                                             vmem_limit_bytes=VMEM_LIMIT_BYTES),
        name="moe_experts",
    )(h, hb, w_dense, eg, eu, ed, sg, su, sd, ln_g, ln_b)


def kernel(x, w_in, swa_sinks, w_branch_swa, w_branch_moba, w_out, ln1_g, ln1_b, w_router, router_bias,
           w_exp_gate, w_exp_up, w_exp_down, w_sh_gate, w_sh_up, w_sh_down, ln2_g, ln2_b):
    B, S, D = x.shape
    depth = w_in.shape[0]
    alpha = (2.0 * depth) ** 0.25
    h = x.reshape(B * S, D)
    for l in range(depth):
        w_in_b = w_in[l].astype(BF16)
        qa, ka, va, qb, kb, vb, kmean = _qkv_proj(h, w_in_b[:, :QKV_COLS])
        att_a = _swa_attention(qa, ka, va, swa_sinks[l], B, S)
        att_b = _moba_attention(qb, kb, vb, kmean.reshape(B, S // MOBA_BLOCK, MOBA_W), B, S)
        h1, h1b, w_dense = _mix_ln_route(
            h, att_a, att_b, w_in_b[:, QKV_COLS:], w_branch_swa[l].astype(BF16), w_branch_moba[l].astype(BF16),
            w_out[l].astype(BF16), ln1_g[l].reshape(1, D), ln1_b[l].reshape(1, D),
            w_router[l].T, router_bias[l].reshape(N_EXPERTS, 1), alpha)
        h = _moe(h1, h1b, w_dense, w_exp_gate[l].astype(BF16), w_exp_up[l].astype(BF16),
                 w_exp_down[l].astype(BF16), w_sh_gate[l].astype(BF16), w_sh_up[l].astype(BF16),
                 w_sh_down[l].astype(BF16), ln2_g[l].reshape(1, D), ln2_b[l].reshape(1, D), alpha)
    return h.reshape(B, S, D)
```

```python
import functools

import numpy as np
import jax
import jax.numpy as jnp
from jax import lax
from jax.experimental import pallas as pl
from jax.experimental.pallas import tpu as pltpu

F32 = jnp.float32
BF16 = jnp.bfloat16

HEAD_DIM = 64
SWA_HEADS = 8
SWA_KV_HEADS = 2
SWA_BLOCK = 128
MOBA_HEADS = 8
MOBA_BLOCK = 256
MOBA_TOPK = 3
N_EXPERTS = 64
TOP_K = 8
N_GROUPS = 8
TOPK_GROUPS = 4
ROUTED_SCALE = 2.5
LN_EPS = 1e-5
NEG_INF = -1e30
REMOVED = -3e38

SWA_Q_W = SWA_HEADS * HEAD_DIM
SWA_KV_W = SWA_KV_HEADS * HEAD_DIM
MOBA_W = MOBA_HEADS * HEAD_DIM
QKV_COLS = SWA_Q_W + 2 * SWA_KV_W + 3 * MOBA_W

_N_SOFTMAX_HEADS = SWA_HEADS + MOBA_HEADS
_SLOPES = np.asarray(2.0 ** (-8.0 * np.arange(1, _N_SOFTMAX_HEADS + 1) / _N_SOFTMAX_HEADS), np.float32)
SWA_SLOPES = [float(s) for s in _SLOPES[:SWA_HEADS]]
MOBA_SLOPES = [float(s) for s in _SLOPES[SWA_HEADS:]]

VMEM_LIMIT_BYTES = 56 * 1024 * 1024


def _dot(a, b):
    return jnp.dot(a, b, preferred_element_type=F32)


def _dot_nt(a, b, precision=None):
    return lax.dot_general(a, b, (((1,), (1,)), ((), ())), preferred_element_type=F32, precision=precision)


def _sigmoid(x):
    return 1.0 / (1.0 + jnp.exp(-x))


def _layernorm(z, g, b):
    mu = jnp.mean(z, axis=-1, keepdims=True)
    zc = z - mu
    var = jnp.mean(zc * zc, axis=-1, keepdims=True)
    return zc * lax.rsqrt(var + LN_EPS) * g + b


def _qkv_kernel(x_ref, w_ref, qa_ref, ka_ref, va_ref, qb_ref, kb_ref, vb_ref, km_ref, *, tm):
    xb = x_ref[...].astype(BF16)
    scale = HEAD_DIM ** -0.5

    def proj(lo, hi):
        return _dot(xb, w_ref[:, lo:hi])

    c = 0
    qa_ref[...] = (proj(c, c + SWA_Q_W) * scale).astype(BF16)
    c += SWA_Q_W
    ka_ref[...] = proj(c, c + SWA_KV_W).astype(BF16)
    c += SWA_KV_W
    va_ref[...] = proj(c, c + SWA_KV_W).astype(BF16)
    c += SWA_KV_W
    qb_ref[...] = (proj(c, c + MOBA_W) * scale).astype(BF16)
    c += MOBA_W
    kb = proj(c, c + MOBA_W)
    kb_ref[...] = kb.astype(BF16)
    for i in range(tm // MOBA_BLOCK):
        blk = kb[i * MOBA_BLOCK:(i + 1) * MOBA_BLOCK, :]
        km_ref[0, i:i + 1, :] = jnp.sum(blk, axis=0, keepdims=True) * (1.0 / MOBA_BLOCK)
    c += MOBA_W
    vb_ref[...] = proj(c, c + MOBA_W).astype(BF16)


def _qkv_proj(xf, w_qkv, tm=512):
    T, D = xf.shape
    nt = T // tm
    row = lambda w: pl.BlockSpec((tm, w), lambda i: (i, 0))
    outs = pl.pallas_call(
        functools.partial(_qkv_kernel, tm=tm),
        grid=(nt,),
        in_specs=[row(D), pl.BlockSpec((D, QKV_COLS), lambda i: (0, 0))],
        out_specs=[row(SWA_Q_W), row(SWA_KV_W), row(SWA_KV_W), row(MOBA_W), row(MOBA_W), row(MOBA_W),
                   pl.BlockSpec((1, tm // MOBA_BLOCK, MOBA_W), lambda i: (i, 0, 0))],
        out_shape=[jax.ShapeDtypeStruct((T, SWA_Q_W), BF16), jax.ShapeDtypeStruct((T, SWA_KV_W), BF16),
                   jax.ShapeDtypeStruct((T, SWA_KV_W), BF16), jax.ShapeDtypeStruct((T, MOBA_W), BF16),
                   jax.ShapeDtypeStruct((T, MOBA_W), BF16), jax.ShapeDtypeStruct((T, MOBA_W), BF16),
                   jax.ShapeDtypeStruct((nt, tm // MOBA_BLOCK, MOBA_W), F32)],
        compiler_params=pltpu.CompilerParams(dimension_semantics=("parallel",), vmem_limit_bytes=VMEM_LIMIT_BYTES),
        name="qkv_proj",
    )(xf, w_qkv)
    return outs


def _swa_kernel(sink_ref, q_ref, kp_ref, ko_ref, vp_ref, vo_ref, o_ref):
    n = pl.program_id(1)
    blk = SWA_BLOCK
    k = jnp.concatenate([kp_ref[...], ko_ref[...]], axis=0)
    v = jnp.concatenate([vp_ref[...], vo_ref[...]], axis=0)
    i = lax.broadcasted_iota(jnp.int32, (blk, 2 * blk), 0)
    j = lax.broadcasted_iota(jnp.int32, (blk, 2 * blk), 1)
    dist = i + blk - j
    valid = (dist >= 0) & (dist < blk) & ((j >= blk) | (n > 0))
    distf = dist.astype(F32)
    group = SWA_HEADS // SWA_KV_HEADS
    outs = []
    for h in range(SWA_HEADS):
        kv = h // group
        qh = q_ref[:, h * HEAD_DIM:(h + 1) * HEAD_DIM]
        kh = k[:, kv * HEAD_DIM:(kv + 1) * HEAD_DIM]
        vh = v[:, kv * HEAD_DIM:(kv + 1) * HEAD_DIM]
        s = _dot_nt(qh, kh)
        s = jnp.where(valid, s - SWA_SLOPES[h] * distf, NEG_INF)
        sink = sink_ref[h]
        m = jnp.maximum(jnp.max(s, axis=-1, keepdims=True), sink)
        e = jnp.exp(s - m)
        denom = jnp.sum(e, axis=-1, keepdims=True) + jnp.exp(sink - m)
        outs.append(_dot(e.astype(BF16), vh) / denom)
    o_ref[...] = jnp.concatenate(outs, axis=-1).astype(BF16)


def _swa_attention(qa, ka, va, sinks, B, S):
    nq = S // SWA_BLOCK
    own = lambda b, n: (b * nq + n, 0)
    prev = lambda b, n: (b * nq + jnp.maximum(n - 1, 0), 0)
    kv_spec = lambda im: pl.BlockSpec((SWA_BLOCK, SWA_KV_W), im)
    return pl.pallas_call(
        _swa_kernel,
        grid=(B, nq),
        in_specs=[pl.BlockSpec(memory_space=pltpu.SMEM),
                  pl.BlockSpec((SWA_BLOCK, SWA_Q_W), own),
                  kv_spec(prev), kv_spec(own), kv_spec(prev), kv_spec(own)],
        out_specs=pl.BlockSpec((SWA_BLOCK, SWA_Q_W), own),
        out_shape=jax.ShapeDtypeStruct((B * S, SWA_Q_W), BF16),
        compiler_params=pltpu.CompilerParams(dimension_semantics=("parallel", "parallel")),
        name="swa_attention",
    )(sinks, qa, ka, ka, va, va)


MOBA_ROWS = 256


def _moba_bias_table(S, nb):
    pos = np.arange(S)
    onehot = np.zeros((S, MOBA_HEADS, HEAD_DIM), np.float32)
    onehot[pos, :, pos // MOBA_BLOCK] = 1.0
    tab = jnp.asarray(onehot)
    rem = jnp.asarray(np.asarray(MOBA_SLOPES, np.float32)[None, :] * pos[:, None].astype(np.float32))
    for t in range(3):
        part = rem.astype(BF16).astype(F32)
        tab = tab.at[:, :, nb + t].set(part)
        rem = rem - part
    return tab.reshape(S, MOBA_HEADS * HEAD_DIM).astype(BF16)


def _moba_kernel(q_ref, k_ref, v_ref, km_ref, tab_ref, o_ref, kaug_sc, s_sc, *, nb):
    qi = pl.program_id(2)
    blk = MOBA_BLOCK
    hd = HEAD_DIM
    rows_per = MOBA_ROWS

    @pl.when(qi == 0)
    def _():
        for hh in range(2):
            kaug_sc[:, 2 * hh * hd:(2 * hh + 1) * hd] = k_ref[:, hh * hd:(hh + 1) * hd]
            kaug_sc[:, (2 * hh + 1) * hd:(2 * hh + 2) * hd] = tab_ref[:, hh * hd:(hh + 1) * hd]

    ri = lax.broadcasted_iota(jnp.int32, (rows_per, blk), 0)
    cj = lax.broadcasted_iota(jnp.int32, (rows_per, blk), 1)
    n_iota = lax.broadcasted_iota(jnp.int32, (nb, blk), 0)
    lane64 = lax.broadcasted_iota(jnp.int32, (1, hd), 1)
    ones_row = jnp.where((lane64 >= nb) & (lane64 < nb + 3), 1.0, 0.0)
    out_lane = lax.broadcasted_iota(jnp.int32, (rows_per, 2 * hd), 1)

    def tile(c):
        q_augs = []
        for hh in range(2):
            lanes = slice(hh * hd, (hh + 1) * hd)
            qh = q_ref[:, lanes]
            g_t = _dot_nt(km_ref[0, :, lanes], qh.astype(F32), precision=lax.Precision.HIGHEST)
            rank = jnp.zeros((nb, blk), jnp.int32)
            for mblk in range(c):
                gm = g_t[mblk:mblk + 1, :]
                beats = (gm > g_t) | ((gm == g_t) & (mblk < n_iota))
                rank = rank + jnp.where(beats, 1, 0)
            keep = ((n_iota < c) & (rank < MOBA_TOPK)) | (n_iota == c)
            selb = jnp.where(keep, 0.0, NEG_INF)
            selb_t = jnp.concatenate([selb, jnp.zeros((128 - nb, blk), F32)], axis=0).T
            q_bias = (selb_t[:, :hd] + ones_row).astype(BF16)
            q_augs.append(jnp.concatenate([qh, q_bias], axis=1))
        res = []
        for hh in range(2):
            q_aug = q_augs[hh]
            per_chunk = []
            for rc in range(blk // rows_per):
                qa = q_aug[rc * rows_per:(rc + 1) * rows_per]
                mx = None
                for n in range(c + 1):
                    s = _dot_nt(qa, kaug_sc[n * blk:(n + 1) * blk, 2 * hh * hd:(2 * hh + 2) * hd])
                    if n == c:
                        s = jnp.where(ri + rc * rows_per >= cj, s, NEG_INF)
                    s_sc[hh, rc, n] = s
                    t = jnp.maximum(s[:, :128], s[:, 128:])
                    mx = t if mx is None else jnp.maximum(mx, t)
                m = jnp.max(mx, axis=-1, keepdims=True)
                lsum = None
                acc = None
                for n in range(c + 1):
                    p = jnp.exp(s_sc[hh, rc, n] - m)
                    t = p[:, :128] + p[:, 128:]
                    lsum = t if lsum is None else lsum + t
                    pv = _dot(p.astype(BF16), v_ref[n * blk:(n + 1) * blk, :])
                    acc = pv if acc is None else acc + pv
                per_chunk.append(acc / jnp.sum(lsum, axis=-1, keepdims=True))
            res.append(per_chunk)
        for rc in range(blk // rows_per):
            o = jnp.where(out_lane < hd, res[0][rc], res[1][rc])
            o_ref[rc * rows_per:(rc + 1) * rows_per, :] = o.astype(BF16)

    for c in range(nb):
        pl.when(qi == c)(functools.partial(tile, c))


def _moba_attention(qb, kb, vb, kmean, B, S):
    nb = S // MOBA_BLOCK
    blk = MOBA_BLOCK
    hpairs = MOBA_HEADS // 2
    pair = 2 * HEAD_DIM
    tab = _moba_bias_table(S, nb)
    return pl.pallas_call(
        functools.partial(_moba_kernel, nb=nb),
        grid=(B, hpairs, nb),
        in_specs=[pl.BlockSpec((blk, pair), lambda b, h, q: (b * nb + q, h)),
                  pl.BlockSpec((S, pair), lambda b, h, q: (b, h)),
                  pl.BlockSpec((S, pair), lambda b, h, q: (b, h)),
                  pl.BlockSpec((1, nb, pair), lambda b, h, q: (b, 0, h)),
                  pl.BlockSpec((S, pair), lambda b, h, q: (0, h))],
        out_specs=pl.BlockSpec((blk, pair), lambda b, h, q: (b * nb + q, h)),
        out_shape=jax.ShapeDtypeStruct((B * S, MOBA_W), BF16),
        scratch_shapes=[pltpu.VMEM((S, 2 * pair), BF16),
                        pltpu.VMEM((2, blk // MOBA_ROWS, nb, MOBA_ROWS, blk), F32)],
        compiler_params=pltpu.CompilerParams(dimension_semantics=("parallel", "parallel", "arbitrary"),
                                             vmem_limit_bytes=VMEM_LIMIT_BYTES),
        name="moba_attention",
    )(qb, kb, vb, kmean, tab)


def _route(scores_t, biased, tm):
    gsz = N_EXPERTS // N_GROUPS
    sub = lax.broadcasted_iota(jnp.int32, (gsz, tm), 0)
    grp = [biased[g * gsz:(g + 1) * gsz, :] for g in range(N_GROUPS)]
    gscore = []
    for v in grp:
        m1 = jnp.max(v, axis=0, keepdims=True)
        first = jnp.min(jnp.where(v == m1, sub, gsz), axis=0, keepdims=True)
        m2 = jnp.max(jnp.where(sub == first, REMOVED, v), axis=0, keepdims=True)
        gscore.append(m1 + m2)
    masked = []
    for g in range(N_GROUPS):
        rank = jnp.zeros((1, tm), jnp.int32)
        for g2 in range(N_GROUPS):
            if g2 == g:
                continue
            beats = (gscore[g2] >= gscore[g]) if g2 < g else (gscore[g2] > gscore[g])
            rank = rank + jnp.where(beats, 1, 0)
        masked.append(jnp.where(rank < TOPK_GROUPS, grp[g], NEG_INF))
    v = jnp.concatenate(masked, axis=0)
    e_iota = lax.broadcasted_iota(jnp.int32, (N_EXPERTS, tm), 0)
    picked = jnp.zeros((N_EXPERTS, tm), F32)
    experts, svals = [], []
    for _ in range(TOP_K):
        m = jnp.max(v, axis=0, keepdims=True)
        first = jnp.min(jnp.where(v == m, e_iota, N_EXPERTS), axis=0, keepdims=True)
        pick = e_iota == first
        experts.append(first)
        svals.append(jnp.sum(jnp.where(pick, scores_t, 0.0), axis=0, keepdims=True))
        picked = jnp.where(pick, 1.0, picked)
        v = jnp.where(pick, REMOVED, v)
    denom = svals[0]
    for s in svals[1:]:
        denom = denom + s
    weights = [s / denom * ROUTED_SCALE for s in svals]
    return experts, weights, picked


MOE_TILE = 2048
ROUTE_SHIFT = 12
assert MOE_TILE <= (1 << ROUTE_SHIFT)


def _mix_kernel(x_ref, aa_ref, ab_ref, wg_ref, wa_ref, wb_ref, wo_ref, g_ref, b_ref, wr_ref, rb_ref, tri_ref,
                h_ref, code_ref, wk_ref, cnt_ref, run_sc, *, alpha, tm):
    step = pl.program_id(0) % (MOE_TILE // tm)
    x = x_ref[...]
    xb = x.astype(BF16)
    d = x.shape[-1]
    ga = _dot(xb, wg_ref[:, :d])
    gb = _dot(xb, wg_ref[:, d:])
    y = _sigmoid(ga) * _dot(aa_ref[...], wa_ref[...]) + _sigmoid(gb) * _dot(ab_ref[...], wb_ref[...])
    mix = _dot(y.astype(BF16), wo_ref[...])
    h = _layernorm(alpha * x + mix, g_ref[...], b_ref[...])
    h_ref[...] = h
    logits_t = _dot_nt(wr_ref[...], h, precision=lax.Precision.HIGHEST)
    scores_t = _sigmoid(logits_t)
    experts, weights, picked = _route(scores_t, scores_t + rb_ref[...], tm)

    @pl.when(step == 0)
    def _():
        run_sc[...] = jnp.zeros_like(run_sc)

    pos = _dot(picked.astype(BF16), tri_ref[...]) + run_sc[...]
    run_sc[...] += jnp.sum(picked, axis=1, keepdims=True)
    e_iota = lax.broadcasted_iota(jnp.int32, (N_EXPERTS, tm), 0)
    codes = []
    for k in range(TOP_K):
        rank = jnp.sum(jnp.where(e_iota == experts[k], pos, 0.0), axis=0, keepdims=True)
        codes.append(experts[k] * (1 << ROUTE_SHIFT) + rank.astype(jnp.int32))
    for c in range(tm // LANES):
        lanes = slice(c * LANES, (c + 1) * LANES)
        code_ref[0, c] = jnp.concatenate([v[:, lanes] for v in codes], axis=0)
        wk_ref[0, c] = jnp.concatenate([v[:, lanes] for v in weights], axis=0)

    picked_t = jnp.concatenate([picked, jnp.zeros((128 - N_EXPERTS, tm), F32)], axis=0).T
    counts = jnp.sum(picked_t, axis=0, keepdims=True).astype(jnp.int32)

    @pl.when(step == 0)
    def _():
        cnt_ref[0] = counts

    @pl.when(step != 0)
    def _():
        cnt_ref[0] += counts


def _mix_ln_route(xf, att_a, att_b, w_gates, w_a, w_b, w_o, ln_g, ln_b, w_router_t, router_bias, alpha, tm=256):
    T, D = xf.shape
    assert MOE_TILE % tm == 0 and T % MOE_TILE == 0
    steps = MOE_TILE // tm
    tri = jnp.asarray(np.triu(np.ones((tm, tm), np.float32), k=1), BF16)
    row = lambda w: pl.BlockSpec((tm, w), lambda i: (i, 0))
    col = pl.BlockSpec((1, tm // LANES, TOP_K, LANES), lambda i: (i // steps, i % steps, 0, 0))
    rec_shape = (T // MOE_TILE, MOE_TILE // LANES, TOP_K, LANES)
    full = lambda a: pl.BlockSpec(a.shape, lambda i: (0,) * a.ndim)
    return pl.pallas_call(
        functools.partial(_mix_kernel, alpha=alpha, tm=tm),
        grid=(T // tm,),
        in_specs=[row(D), row(SWA_Q_W), row(MOBA_W), full(w_gates), full(w_a), full(w_b), full(w_o),
                  full(ln_g), full(ln_b), full(w_router_t), full(router_bias), full(tri)],
        out_specs=[row(D), col, col, pl.BlockSpec((1, 1, 128), lambda i: (i // steps, 0, 0))],
        out_shape=[jax.ShapeDtypeStruct((T, D), F32), jax.ShapeDtypeStruct(rec_shape, jnp.int32),
                   jax.ShapeDtypeStruct(rec_shape, F32), jax.ShapeDtypeStruct((T // MOE_TILE, 1, 128), jnp.int32)],
        scratch_shapes=[pltpu.VMEM((N_EXPERTS, 1), F32)],
        compiler_params=pltpu.CompilerParams(dimension_semantics=("arbitrary",), vmem_limit_bytes=VMEM_LIMIT_BYTES),
        name="mix_ln_route",
    )(xf, att_a, att_b, w_gates, w_a, w_b, w_o, ln_g, ln_b, w_router_t, router_bias, tri)


LANES = 128
XY_STRIDE = MOE_TILE + 8
MOE_ROWS = 64
MOE_STATIC_BLOCKS = 12
EDGE_ROWS = 256
GATHER_UNROLL = 8
SCATTER_UNROLL = 8


def _moe_kernel(h_ref, code_ref, wk_ref, cnt_ref, eg_ref, eu_ref, ed_ref, sg_ref, su_ref, sd_ref, g_ref, b_ref,
                o_ref, acc_sc, xy_sc, dst_sc, dst_s, wk_s, cnt_s, off_s, ltok_s, lw_s, sem, *, alpha, ch):
    tile = pl.program_id(0)
    e = pl.program_id(1)
    tt = MOE_TILE

    def token_major_rows(ref, r0, rows):
        start = pl.multiple_of(r0 * ch, ch)
        return jnp.concatenate([ref[pl.ds(start + j, rows, stride=ch), :] for j in range(ch)], axis=1)

    def swiglu(xb, wg, wu, wd):
        a = _dot(xb, wg)
        return _dot((a * _sigmoid(a) * _dot(xb, wu)).astype(BF16), wd)

    @pl.when((tile == 0) & (e == 0))
    def _():
        xy_sc[...] = jnp.zeros_like(xy_sc)

    @pl.when(e == 0)
    def _():
        copies = [pltpu.make_async_copy(wk_ref.at[0], wk_s, sem.at[1]),
                  pltpu.make_async_copy(cnt_ref.at[0], cnt_s, sem.at[2])]
        for c in copies:
            c.start()

        def shared_block(rb, carry):
            r0 = rb * EDGE_ROWS
            y = swiglu(token_major_rows(h_ref, r0, EDGE_ROWS).astype(BF16), sg_ref[...], su_ref[...], sd_ref[...])
            start = pl.multiple_of(r0 * ch, ch)
            for j in range(ch):
                acc_sc[pl.ds(start + j, EDGE_ROWS, stride=ch), :] = y[:, j * LANES:(j + 1) * LANES]
            return carry

        lax.fori_loop(0, tt // EDGE_ROWS, shared_block, 0)
        for c in copies:
            c.wait()

        total = jnp.int32(0)
        for x in range(N_EXPERTS):
            off_s[x] = total
            total = total + cnt_s[0, x]
        off_s[N_EXPERTS] = total
        code = code_ref[0]
        expert = code >> ROUTE_SHIFT
        dst = code & ((1 << ROUTE_SHIFT) - 1)
        for x in range(1, N_EXPERTS):
            dst = dst + jnp.where(expert == x, off_s[x], 0)
        dst_sc[...] = dst
        dst_copy = pltpu.make_async_copy(dst_sc, dst_s, sem.at[0])
        dst_copy.start()
        dst_copy.wait()

        def build(c, carry):
            for k in range(TOP_K):
                for u in range(LANES):
                    d = dst_s[c, k, u]
                    ltok_s[d] = (c * LANES + u) * ch
                    lw_s[d] = wk_s[c, k, u]
            return carry

        lax.fori_loop(0, tt // LANES, build, 0)

    base = off_s[e]
    n = off_s[e + 1] - base

    def gather(m0, count):
        for u in range(count):
            m = m0 + u
            t8 = pl.multiple_of(ltok_s[base + m], ch)
            xy_sc[pl.ds(m, ch, stride=XY_STRIDE), :] = h_ref[pl.ds(t8, ch), :]

    n_main = (n // GATHER_UNROLL) * GATHER_UNROLL
    lax.fori_loop(0, n // GATHER_UNROLL, lambda g, c: (gather(g * GATHER_UNROLL, GATHER_UNROLL), c)[1], 0)
    lax.fori_loop(n_main, n, lambda m, c: (gather(m, 1), c)[1], 0)

    def expert_rows(r0, rows):
        xb = jnp.concatenate([xy_sc[pl.ds(j * XY_STRIDE + r0, rows), :] for j in range(ch)], axis=1).astype(BF16)
        y = swiglu(xb, eg_ref[0], eu_ref[0], ed_ref[0])
        for j in range(ch):
            xy_sc[pl.ds(j * XY_STRIDE + r0, rows), :] = y[:, j * LANES:(j + 1) * LANES]

    n_blocks = (n + MOE_ROWS - 1) // MOE_ROWS
    for v in range(1, MOE_STATIC_BLOCKS + 1):
        pl.when(n_blocks == v)(functools.partial(expert_rows, 0, v * MOE_ROWS))

    @pl.when(n_blocks > MOE_STATIC_BLOCKS)
    def _():
        def expert_block(rb, carry):
            expert_rows(pl.multiple_of(rb * MOE_ROWS, MOE_ROWS), MOE_ROWS)
            return carry

        lax.fori_loop(0, n_blocks, expert_block, 0)

    def scatter(m0, count):
        new = []
        for u in range(count):
            m = m0 + u
            t8 = pl.multiple_of(ltok_s[base + m], ch)
            y = xy_sc[pl.ds(m, ch, stride=XY_STRIDE), :]
            new.append((t8, acc_sc[pl.ds(t8, ch), :] + lw_s[base + m] * y))
        for t8, v in new:
            acc_sc[pl.ds(t8, ch), :] = v

    n_main = (n // SCATTER_UNROLL) * SCATTER_UNROLL
    lax.fori_loop(0, n // SCATTER_UNROLL, lambda g, c: (scatter(g * SCATTER_UNROLL, SCATTER_UNROLL), c)[1], 0)
    lax.fori_loop(n_main, n, lambda m, c: (scatter(m, 1), c)[1], 0)

    @pl.when(e == pl.num_programs(1) - 1)
    def _():
        def final_block(rb, carry):
            r0 = rb * EDGE_ROWS
            z = alpha * token_major_rows(h_ref, r0, EDGE_ROWS) + token_major_rows(acc_sc, r0, EDGE_ROWS)
            o_ref[pl.ds(pl.multiple_of(r0, EDGE_ROWS), EDGE_ROWS), :] = _layernorm(z, g_ref[...], b_ref[...])
            return carry

        lax.fori_loop(0, tt // EDGE_ROWS, final_block, 0)


def _moe(h, code, wk, cnt, eg, eu, ed, sg, su, sd, ln_g, ln_b, alpha):
    T, D = h.shape
    E, _, F = eg.shape
    ch = D // LANES
    assert ch == 8, "a token must fill exactly one 8-sublane group"
    tt = MOE_TILE
    h_tm = h.reshape(T * ch, LANES)
    full = lambda a: pl.BlockSpec(a.shape, lambda i, e: (0,) * a.ndim)
    return pl.pallas_call(
        functools.partial(_moe_kernel, alpha=alpha, ch=ch),
        grid=(T // tt, E),
        in_specs=[pl.BlockSpec((tt * ch, LANES), lambda i, e: (i, 0), pipeline_mode=pl.Buffered(1)),
                  pl.BlockSpec((1, tt // LANES, TOP_K, LANES), lambda i, e: (i, 0, 0, 0)),
                  pl.BlockSpec((1, tt // LANES, TOP_K, LANES), lambda i, e: (i, 0, 0, 0)),
                  pl.BlockSpec((1, 1, 128), lambda i, e: (i, 0, 0)),
                  pl.BlockSpec((1, D, F), lambda i, e: (e, 0, 0)),
                  pl.BlockSpec((1, D, F), lambda i, e: (e, 0, 0)),
                  pl.BlockSpec((1, F, D), lambda i, e: (e, 0, 0)),
                  full(sg), full(su), full(sd), full(ln_g), full(ln_b)],
        out_specs=pl.BlockSpec((tt, D), lambda i, e: (i, 0)),
        out_shape=jax.ShapeDtypeStruct((T, D), F32),
        scratch_shapes=[pltpu.VMEM((tt * ch, LANES), F32),
                        pltpu.VMEM((ch * XY_STRIDE, LANES), F32),
                        pltpu.VMEM((tt // LANES, TOP_K, LANES), jnp.int32),
                        pltpu.SMEM((tt // LANES, TOP_K, LANES), jnp.int32),
                        pltpu.SMEM((tt // LANES, TOP_K, LANES), F32),
                        pltpu.SMEM((1, 128), jnp.int32),
                        pltpu.SMEM((128,), jnp.int32),
                        pltpu.SMEM((TOP_K * tt,), jnp.int32),
                        pltpu.SMEM((TOP_K * tt,), F32),
                        pltpu.SemaphoreType.DMA((3,))],
        compiler_params=pltpu.CompilerParams(dimension_semantics=("arbitrary", "arbitrary"),
                                             vmem_limit_bytes=VMEM_LIMIT_BYTES),
        name="moe_experts",
    )(h_tm, code, wk, cnt, eg, eu, ed, sg, su, sd, ln_g, ln_b)


def kernel(x, w_in, swa_sinks, w_branch_swa, w_branch_moba, w_out, ln1_g, ln1_b, w_router, router_bias,
           w_exp_gate, w_exp_up, w_exp_down, w_sh_gate, w_sh_up, w_sh_down, ln2_g, ln2_b):
    B, S, D = x.shape
    depth = w_in.shape[0]
    alpha = (2.0 * depth) ** 0.25
    h = x.reshape(B * S, D)
    for l in range(depth):
        w_in_b = w_in[l].astype(BF16)
        qa, ka, va, qb, kb, vb, kmean = _qkv_proj(h, w_in_b[:, :QKV_COLS])
        att_a = _swa_attention(qa, ka, va, swa_sinks[l], B, S)
        att_b = _moba_attention(qb, kb, vb, kmean.reshape(B, S // MOBA_BLOCK, MOBA_W), B, S)
        h1, code, wk, cnt = _mix_ln_route(
            h, att_a, att_b, w_in_b[:, QKV_COLS:], w_branch_swa[l].astype(BF16), w_branch_moba[l].astype(BF16),
            w_out[l].astype(BF16), ln1_g[l].reshape(1, D), ln1_b[l].reshape(1, D),
            w_router[l].T, router_bias[l].reshape(N_EXPERTS, 1), alpha)
        h = _moe(h1, code, wk, cnt, w_exp_gate[l].astype(BF16), w_exp_up[l].astype(BF16),
                 w_exp_down[l].astype(BF16), w_sh_gate[l].astype(BF16), w_sh_up[l].astype(BF16),
                 w_sh_down[l].astype(BF16), ln2_g[l].reshape(1, D), ln2_b[l].reshape(1, D), alpha)
    return h.reshape(B, S, D)
```

```python
import functools

import numpy as np
import jax
import jax.numpy as jnp
from jax import lax
from jax.experimental import pallas as pl
from jax.experimental.pallas import tpu as pltpu

F32 = jnp.float32
BF16 = jnp.bfloat16

HEAD_DIM = 64
SWA_HEADS = 8
SWA_KV_HEADS = 2
SWA_BLOCK = 128
MOBA_HEADS = 8
MOBA_BLOCK = 256
MOBA_TOPK = 3
N_EXPERTS = 64
TOP_K = 8
N_GROUPS = 8
TOPK_GROUPS = 4
ROUTED_SCALE = 2.5
LN_EPS = 1e-5
NEG_INF = -1e30
REMOVED = -3e38

SWA_Q_W = SWA_HEADS * HEAD_DIM
SWA_KV_W = SWA_KV_HEADS * HEAD_DIM
MOBA_W = MOBA_HEADS * HEAD_DIM
QKV_COLS = SWA_Q_W + 2 * SWA_KV_W + 3 * MOBA_W

_N_SOFTMAX_HEADS = SWA_HEADS + MOBA_HEADS
_SLOPES = np.asarray(2.0 ** (-8.0 * np.arange(1, _N_SOFTMAX_HEADS + 1) / _N_SOFTMAX_HEADS), np.float32)
SWA_SLOPES = [float(s) for s in _SLOPES[:SWA_HEADS]]
MOBA_SLOPES = [float(s) for s in _SLOPES[SWA_HEADS:]]

LANES = 128
VMEM_LIMIT_BYTES = 56 * 1024 * 1024


def _dot(a, b):
    return jnp.dot(a, b, preferred_element_type=F32)


def _dot_nt(a, b, precision=None):
    return lax.dot_general(a, b, (((1,), (1,)), ((), ())), preferred_element_type=F32, precision=precision)


def _sigmoid(x):
    return 1.0 / (1.0 + jnp.exp(-x))


def _layernorm(z, g, b):
    mu = jnp.mean(z, axis=-1, keepdims=True)
    zc = z - mu
    var = jnp.mean(zc * zc, axis=-1, keepdims=True)
    return zc * lax.rsqrt(var + LN_EPS) * g + b


def _qkv_kernel(x_ref, w_ref, qa_ref, ka_ref, va_ref, qb_ref, kb_ref, vb_ref, km_ref, *, tm):
    xb = x_ref[...].astype(BF16)
    scale = HEAD_DIM ** -0.5

    def proj(lo, hi):
        return _dot(xb, w_ref[:, lo:hi])

    c = 0
    qa_ref[...] = (proj(c, c + SWA_Q_W) * scale).astype(BF16)
    c += SWA_Q_W
    ka_ref[...] = proj(c, c + SWA_KV_W).astype(BF16)
    c += SWA_KV_W
    va_ref[...] = proj(c, c + SWA_KV_W).astype(BF16)
    c += SWA_KV_W
    qb_ref[...] = (proj(c, c + MOBA_W) * scale).astype(BF16)
    c += MOBA_W
    kb = proj(c, c + MOBA_W)
    kb_ref[...] = kb.astype(BF16)
    for i in range(tm // MOBA_BLOCK):
        blk = kb[i * MOBA_BLOCK:(i + 1) * MOBA_BLOCK, :]
        km_ref[0, i:i + 1, :] = jnp.sum(blk, axis=0, keepdims=True) * (1.0 / MOBA_BLOCK)
    c += MOBA_W
    vb_ref[...] = proj(c, c + MOBA_W).astype(BF16)


def _qkv_proj(xf, w_qkv, tm=512):
    T, D = xf.shape
    nt = T // tm
    row = lambda w: pl.BlockSpec((tm, w), lambda i: (i, 0))
    outs = pl.pallas_call(
        functools.partial(_qkv_kernel, tm=tm),
        grid=(nt,),
        in_specs=[row(D), pl.BlockSpec((D, QKV_COLS), lambda i: (0, 0))],
        out_specs=[row(SWA_Q_W), row(SWA_KV_W), row(SWA_KV_W), row(MOBA_W), row(MOBA_W), row(MOBA_W),
                   pl.BlockSpec((1, tm // MOBA_BLOCK, MOBA_W), lambda i: (i, 0, 0))],
        out_shape=[jax.ShapeDtypeStruct((T, SWA_Q_W), BF16), jax.ShapeDtypeStruct((T, SWA_KV_W), BF16),
                   jax.ShapeDtypeStruct((T, SWA_KV_W), BF16), jax.ShapeDtypeStruct((T, MOBA_W), BF16),
                   jax.ShapeDtypeStruct((T, MOBA_W), BF16), jax.ShapeDtypeStruct((T, MOBA_W), BF16),
                   jax.ShapeDtypeStruct((nt, tm // MOBA_BLOCK, MOBA_W), F32)],
        compiler_params=pltpu.CompilerParams(dimension_semantics=("parallel",), vmem_limit_bytes=VMEM_LIMIT_BYTES),
        name="qkv_proj",
    )(xf, w_qkv)
    return outs


def _swa_kernel(sink_ref, q_ref, kp_ref, ko_ref, vp_ref, vo_ref, o_ref):
    n = pl.program_id(1)
    blk = SWA_BLOCK
    k = jnp.concatenate([kp_ref[...], ko_ref[...]], axis=0)
    v = jnp.concatenate([vp_ref[...], vo_ref[...]], axis=0)
    i = lax.broadcasted_iota(jnp.int32, (blk, 2 * blk), 0)
    j = lax.broadcasted_iota(jnp.int32, (blk, 2 * blk), 1)
    dist = i + blk - j
    valid = (dist >= 0) & (dist < blk) & ((j >= blk) | (n > 0))
    distf = dist.astype(F32)
    group = SWA_HEADS // SWA_KV_HEADS
    outs = []
    for h in range(SWA_HEADS):
        kv = h // group
        qh = q_ref[:, h * HEAD_DIM:(h + 1) * HEAD_DIM]
        kh = k[:, kv * HEAD_DIM:(kv + 1) * HEAD_DIM]
        vh = v[:, kv * HEAD_DIM:(kv + 1) * HEAD_DIM]
        s = _dot_nt(qh, kh)
        s = jnp.where(valid, s - SWA_SLOPES[h] * distf, NEG_INF)
        sink = sink_ref[h]
        m = jnp.maximum(jnp.max(s, axis=-1, keepdims=True), sink)
        e = jnp.exp(s - m)
        denom = jnp.sum(e, axis=-1, keepdims=True) + jnp.exp(sink - m)
        outs.append(_dot(e.astype(BF16), vh) / denom)
    o_ref[...] = jnp.concatenate(outs, axis=-1).astype(BF16)


def _swa_attention(qa, ka, va, sinks, B, S):
    nq = S // SWA_BLOCK
    own = lambda b, n: (b * nq + n, 0)
    prev = lambda b, n: (b * nq + jnp.maximum(n - 1, 0), 0)
    kv_spec = lambda im: pl.BlockSpec((SWA_BLOCK, SWA_KV_W), im)
    return pl.pallas_call(
        _swa_kernel,
        grid=(B, nq),
        in_specs=[pl.BlockSpec(memory_space=pltpu.SMEM),
                  pl.BlockSpec((SWA_BLOCK, SWA_Q_W), own),
                  kv_spec(prev), kv_spec(own), kv_spec(prev), kv_spec(own)],
        out_specs=pl.BlockSpec((SWA_BLOCK, SWA_Q_W), own),
        out_shape=jax.ShapeDtypeStruct((B * S, SWA_Q_W), BF16),
        compiler_params=pltpu.CompilerParams(dimension_semantics=("parallel", "parallel")),
        name="swa_attention",
    )(sinks, qa, ka, ka, va, va)


MOBA_ROWS = 256


def _moba_bias_table(S, nb):
    pos = np.arange(S)
    onehot = np.zeros((S, MOBA_HEADS, HEAD_DIM), np.float32)
    onehot[pos, :, pos // MOBA_BLOCK] = 1.0
    tab = jnp.asarray(onehot)
    rem = jnp.asarray(np.asarray(MOBA_SLOPES, np.float32)[None, :] * pos[:, None].astype(np.float32))
    for t in range(3):
        part = rem.astype(BF16).astype(F32)
        tab = tab.at[:, :, nb + t].set(part)
        rem = rem - part
    return tab.reshape(S, MOBA_HEADS * HEAD_DIM).astype(BF16)


def _moba_kernel(q_ref, k_ref, v_ref, km_ref, tab_ref, o_ref, kaug_sc, s_sc, *, nb):
    qi = pl.program_id(2)
    blk = MOBA_BLOCK
    hd = HEAD_DIM
    rows_per = MOBA_ROWS

    @pl.when(qi == 0)
    def _():
        for hh in range(2):
            kaug_sc[:, 2 * hh * hd:(2 * hh + 1) * hd] = k_ref[:, hh * hd:(hh + 1) * hd]
            kaug_sc[:, (2 * hh + 1) * hd:(2 * hh + 2) * hd] = tab_ref[:, hh * hd:(hh + 1) * hd]

    ri = lax.broadcasted_iota(jnp.int32, (rows_per, blk), 0)
    cj = lax.broadcasted_iota(jnp.int32, (rows_per, blk), 1)
    n_iota = lax.broadcasted_iota(jnp.int32, (nb, blk), 0)
    lane64 = lax.broadcasted_iota(jnp.int32, (1, hd), 1)
    ones_row = jnp.where((lane64 >= nb) & (lane64 < nb + 3), 1.0, 0.0)
    out_lane = lax.broadcasted_iota(jnp.int32, (rows_per, 2 * hd), 1)

    def tile(c):
        q_augs = []
        for hh in range(2):
            lanes = slice(hh * hd, (hh + 1) * hd)
            qh = q_ref[:, lanes]
            g_t = _dot_nt(km_ref[0, :, lanes], qh.astype(F32), precision=lax.Precision.HIGHEST)
            rank = jnp.zeros((nb, blk), jnp.int32)
            for mblk in range(c):
                gm = g_t[mblk:mblk + 1, :]
                beats = (gm > g_t) | ((gm == g_t) & (mblk < n_iota))
                rank = rank + jnp.where(beats, 1, 0)
            keep = ((n_iota < c) & (rank < MOBA_TOPK)) | (n_iota == c)
            selb = jnp.where(keep, 0.0, NEG_INF)
            selb_t = jnp.concatenate([selb, jnp.zeros((128 - nb, blk), F32)], axis=0).T
            q_bias = (selb_t[:, :hd] + ones_row).astype(BF16)
            q_augs.append(jnp.concatenate([qh, q_bias], axis=1))
        res = []
        for hh in range(2):
            q_aug = q_augs[hh]
            per_chunk = []
            for rc in range(blk // rows_per):
                qa = q_aug[rc * rows_per:(rc + 1) * rows_per]
                mx = None
                for n in range(c + 1):
                    s = _dot_nt(qa, kaug_sc[n * blk:(n + 1) * blk, 2 * hh * hd:(2 * hh + 2) * hd])
                    if n == c:
                        s = jnp.where(ri + rc * rows_per >= cj, s, NEG_INF)
                    s_sc[hh, rc, n] = s
                    t = jnp.maximum(s[:, :128], s[:, 128:])
                    mx = t if mx is None else jnp.maximum(mx, t)
                m = jnp.max(mx, axis=-1, keepdims=True)
                lsum = None
                acc = None
                for n in range(c + 1):
                    p = jnp.exp(s_sc[hh, rc, n] - m)
                    t = p[:, :128] + p[:, 128:]
                    lsum = t if lsum is None else lsum + t
                    pv = _dot(p.astype(BF16), v_ref[n * blk:(n + 1) * blk, :])
                    acc = pv if acc is None else acc + pv
                per_chunk.append(acc / jnp.sum(lsum, axis=-1, keepdims=True))
            res.append(per_chunk)
        for rc in range(blk // rows_per):
            o = jnp.where(out_lane < hd, res[0][rc], res[1][rc])
            o_ref[rc * rows_per:(rc + 1) * rows_per, :] = o.astype(BF16)

    for c in range(nb):
        pl.when(qi == c)(functools.partial(tile, c))


def _moba_attention(qb, kb, vb, kmean, B, S):
    nb = S // MOBA_BLOCK
    blk = MOBA_BLOCK
    hpairs = MOBA_HEADS // 2
    pair = 2 * HEAD_DIM
    tab = _moba_bias_table(S, nb)
    return pl.pallas_call(
        functools.partial(_moba_kernel, nb=nb),
        grid=(B, hpairs, nb),
        in_specs=[pl.BlockSpec((blk, pair), lambda b, h, q: (b * nb + q, h)),
                  pl.BlockSpec((S, pair), lambda b, h, q: (b, h)),
                  pl.BlockSpec((S, pair), lambda b, h, q: (b, h)),
                  pl.BlockSpec((1, nb, pair), lambda b, h, q: (b, 0, h)),
                  pl.BlockSpec((S, pair), lambda b, h, q: (0, h))],
        out_specs=pl.BlockSpec((blk, pair), lambda b, h, q: (b * nb + q, h)),
        out_shape=jax.ShapeDtypeStruct((B * S, MOBA_W), BF16),
        scratch_shapes=[pltpu.VMEM((S, 2 * pair), BF16),
                        pltpu.VMEM((2, blk // MOBA_ROWS, nb, MOBA_ROWS, blk), F32)],
        compiler_params=pltpu.CompilerParams(dimension_semantics=("parallel", "parallel", "arbitrary"),
                                             vmem_limit_bytes=VMEM_LIMIT_BYTES),
        name="moba_attention",
    )(qb, kb, vb, kmean, tab)


def _route(scores_t, biased, tm):
    gsz = N_EXPERTS // N_GROUPS
    sub = lax.broadcasted_iota(jnp.int32, (gsz, tm), 0)
    grp = [biased[g * gsz:(g + 1) * gsz, :] for g in range(N_GROUPS)]
    gscore = []
    for v in grp:
        m1 = jnp.max(v, axis=0, keepdims=True)
        first = jnp.min(jnp.where(v == m1, sub, gsz), axis=0, keepdims=True)
        m2 = jnp.max(jnp.where(sub == first, REMOVED, v), axis=0, keepdims=True)
        gscore.append(m1 + m2)
    masked = []
    for g in range(N_GROUPS):
        rank = jnp.zeros((1, tm), jnp.int32)
        for g2 in range(N_GROUPS):
            if g2 == g:
                continue
            beats = (gscore[g2] >= gscore[g]) if g2 < g else (gscore[g2] > gscore[g])
            rank = rank + jnp.where(beats, 1, 0)
        masked.append(jnp.where(rank < TOPK_GROUPS, grp[g], NEG_INF))
    v = jnp.concatenate(masked, axis=0)
    e_iota = lax.broadcasted_iota(jnp.int32, (N_EXPERTS, tm), 0)
    picked = jnp.zeros((N_EXPERTS, tm), F32)
    experts, svals = [], []
    for _ in range(TOP_K):
        m = jnp.max(v, axis=0, keepdims=True)
        first = jnp.min(jnp.where(v == m, e_iota, N_EXPERTS), axis=0, keepdims=True)
        pick = e_iota == first
        experts.append(first)
        svals.append(jnp.sum(jnp.where(pick, scores_t, 0.0), axis=0, keepdims=True))
        picked = jnp.where(pick, 1.0, picked)
        v = jnp.where(pick, REMOVED, v)
    denom = svals[0]
    for s in svals[1:]:
        denom = denom + s
    weights = [s / denom * ROUTED_SCALE for s in svals]
    return experts, weights, picked


MOE_TILE = 2048
ROUTE_SHIFT = 12
assert MOE_TILE <= (1 << ROUTE_SHIFT)


def _mix_kernel(x_ref, aa_ref, ab_ref, wg_ref, wa_ref, wb_ref, wo_ref, g_ref, b_ref, wr_ref, rb_ref, tri_ref,
                h_ref, code_ref, wk_ref, cnt_ref, run_sc, *, alpha, tm):
    step = pl.program_id(0) % (MOE_TILE // tm)
    x = x_ref[...]
    xb = x.astype(BF16)
    d = x.shape[-1]
    ga = _dot(xb, wg_ref[:, :d])
    gb = _dot(xb, wg_ref[:, d:])
    y = _sigmoid(ga) * _dot(aa_ref[...], wa_ref[...]) + _sigmoid(gb) * _dot(ab_ref[...], wb_ref[...])
    mix = _dot(y.astype(BF16), wo_ref[...])
    h = _layernorm(alpha * x + mix, g_ref[...], b_ref[...])
    for j in range(d // LANES):
        h_ref[pl.ds(j, tm, stride=d // LANES), :] = h[:, j * LANES:(j + 1) * LANES]
    logits_t = _dot_nt(wr_ref[...], h, precision=lax.Precision.HIGHEST)
    scores_t = _sigmoid(logits_t)
    experts, weights, picked = _route(scores_t, scores_t + rb_ref[...], tm)

    @pl.when(step == 0)
    def _():
        run_sc[...] = jnp.zeros_like(run_sc)

    pos = _dot(picked.astype(BF16), tri_ref[...]) + run_sc[...]
    run_sc[...] += jnp.sum(picked, axis=1, keepdims=True)
    e_iota = lax.broadcasted_iota(jnp.int32, (N_EXPERTS, tm), 0)
    codes = []
    for k in range(TOP_K):
        rank = jnp.sum(jnp.where(e_iota == experts[k], pos, 0.0), axis=0, keepdims=True)
        codes.append(experts[k] * (1 << ROUTE_SHIFT) + rank.astype(jnp.int32))
    for c in range(tm // LANES):
        lanes = slice(c * LANES, (c + 1) * LANES)
        code_ref[0, c] = jnp.concatenate([v[:, lanes] for v in codes], axis=0)
        wk_ref[0, c] = jnp.concatenate([v[:, lanes] for v in weights], axis=0)

    picked_t = jnp.concatenate([picked, jnp.zeros((128 - N_EXPERTS, tm), F32)], axis=0).T
    counts = jnp.sum(picked_t, axis=0, keepdims=True).astype(jnp.int32)

    @pl.when(step == 0)
    def _():
        cnt_ref[0] = counts

    @pl.when(step != 0)
    def _():
        cnt_ref[0] += counts


def _mix_ln_route(xf, att_a, att_b, w_gates, w_a, w_b, w_o, ln_g, ln_b, w_router_t, router_bias, alpha, tm=512):
    T, D = xf.shape
    assert MOE_TILE % tm == 0 and T % MOE_TILE == 0
    steps = MOE_TILE // tm
    tri = jnp.asarray(np.triu(np.ones((tm, tm), np.float32), k=1), BF16)
    row = lambda w: pl.BlockSpec((tm, w), lambda i: (i, 0))
    col = pl.BlockSpec((1, tm // LANES, TOP_K, LANES), lambda i: (i // steps, i % steps, 0, 0))
    rec_shape = (T // MOE_TILE, MOE_TILE // LANES, TOP_K, LANES)
    full = lambda a: pl.BlockSpec(a.shape, lambda i: (0,) * a.ndim)
    return pl.pallas_call(
        functools.partial(_mix_kernel, alpha=alpha, tm=tm),
        grid=(T // tm,),
        in_specs=[row(D), row(SWA_Q_W), row(MOBA_W), full(w_gates), full(w_a), full(w_b), full(w_o),
                  full(ln_g), full(ln_b), full(w_router_t), full(router_bias), full(tri)],
        out_specs=[pl.BlockSpec((tm * (D // LANES), LANES), lambda i: (i, 0)), col, col, pl.BlockSpec((1, 1, 128), lambda i: (i // steps, 0, 0))],
        out_shape=[jax.ShapeDtypeStruct((T * (D // LANES), LANES), F32), jax.ShapeDtypeStruct(rec_shape, jnp.int32),
                   jax.ShapeDtypeStruct(rec_shape, F32), jax.ShapeDtypeStruct((T // MOE_TILE, 1, 128), jnp.int32)],
        scratch_shapes=[pltpu.VMEM((N_EXPERTS, 1), F32)],
        compiler_params=pltpu.CompilerParams(dimension_semantics=("arbitrary",), vmem_limit_bytes=VMEM_LIMIT_BYTES),
        name="mix_ln_route",
    )(xf, att_a, att_b, w_gates, w_a, w_b, w_o, ln_g, ln_b, w_router_t, router_bias, tri)


XY_STRIDE = MOE_TILE + 8
MOE_ROWS = 64
MOE_STATIC_BLOCKS = 12
EDGE_ROWS = 256
GATHER_UNROLL = 8
SCATTER_UNROLL = 8


def _moe_kernel(h_ref, code_ref, wk_ref, cnt_ref, eg_ref, eu_ref, ed_ref, sg_ref, su_ref, sd_ref, g_ref, b_ref,
                o_ref, acc_sc, xy_sc, dst_sc, dst_s, wk_s, cnt_s, off_s, ltok_s, lw_s, sem, *, alpha, ch):
    tile = pl.program_id(0)
    e = pl.program_id(1)
    tt = MOE_TILE

    def token_major_rows(ref, r0, rows):
        start = pl.multiple_of(r0 * ch, ch)
        return jnp.concatenate([ref[pl.ds(start + j, rows, stride=ch), :] for j in range(ch)], axis=1)

    def swiglu(xb, wg, wu, wd):
        a = _dot(xb, wg)
        return _dot((a * _sigmoid(a) * _dot(xb, wu)).astype(BF16), wd)

    @pl.when((tile == 0) & (e == 0))
    def _():
        xy_sc[...] = jnp.zeros_like(xy_sc)

    @pl.when(e == 0)
    def _():
        copies = [pltpu.make_async_copy(wk_ref.at[0], wk_s, sem.at[1]),
                  pltpu.make_async_copy(cnt_ref.at[0], cnt_s, sem.at[2])]
        for c in copies:
            c.start()

        def shared_block(rb, carry):
            r0 = rb * EDGE_ROWS
            y = swiglu(token_major_rows(h_ref, r0, EDGE_ROWS).astype(BF16), sg_ref[...], su_ref[...], sd_ref[...])
            start = pl.multiple_of(r0 * ch, ch)
            for j in range(ch):
                acc_sc[pl.ds(start + j, EDGE_ROWS, stride=ch), :] = y[:, j * LANES:(j + 1) * LANES]
            return carry

        lax.fori_loop(0, tt // EDGE_ROWS, shared_block, 0)
        for c in copies:
            c.wait()

        total = jnp.int32(0)
        for x in range(N_EXPERTS):
            off_s[x] = total
            total = total + cnt_s[0, x]
        off_s[N_EXPERTS] = total
        code = code_ref[0]
        expert = code >> ROUTE_SHIFT
        dst = code & ((1 << ROUTE_SHIFT) - 1)
        for x in range(1, N_EXPERTS):
            dst = dst + jnp.where(expert == x, off_s[x], 0)
        dst_sc[...] = dst
        dst_copy = pltpu.make_async_copy(dst_sc, dst_s, sem.at[0])
        dst_copy.start()
        dst_copy.wait()

        def build(c, carry):
            for k in range(TOP_K):
                for u in range(LANES):
                    d = dst_s[c, k, u]
                    ltok_s[d] = (c * LANES + u) * ch
                    lw_s[d] = wk_s[c, k, u]
            return carry

        lax.fori_loop(0, tt // LANES, build, 0)

    base = off_s[e]
    n = off_s[e + 1] - base

    def gather(m0, count):
        for u in range(count):
            m = m0 + u
            t8 = pl.multiple_of(ltok_s[base + m], ch)
            xy_sc[pl.ds(m, ch, stride=XY_STRIDE), :] = h_ref[pl.ds(t8, ch), :]

    n_main = (n // GATHER_UNROLL) * GATHER_UNROLL
    lax.fori_loop(0, n // GATHER_UNROLL, lambda g, c: (gather(g * GATHER_UNROLL, GATHER_UNROLL), c)[1], 0)
    lax.fori_loop(n_main, n, lambda m, c: (gather(m, 1), c)[1], 0)

    def expert_rows(r0, rows):
        xb = jnp.concatenate([xy_sc[pl.ds(j * XY_STRIDE + r0, rows), :] for j in range(ch)], axis=1).astype(BF16)
        y = swiglu(xb, eg_ref[0], eu_ref[0], ed_ref[0])
        for j in range(ch):
            xy_sc[pl.ds(j * XY_STRIDE + r0, rows), :] = y[:, j * LANES:(j + 1) * LANES]

    n_blocks = (n + MOE_ROWS - 1) // MOE_ROWS
    for v in range(1, MOE_STATIC_BLOCKS + 1):
        pl.when(n_blocks == v)(functools.partial(expert_rows, 0, v * MOE_ROWS))

    @pl.when(n_blocks > MOE_STATIC_BLOCKS)
    def _():
        def expert_block(rb, carry):
            expert_rows(pl.multiple_of(rb * MOE_ROWS, MOE_ROWS), MOE_ROWS)
            return carry

        lax.fori_loop(0, n_blocks, expert_block, 0)

    def scatter(m0, count):
        new = []
        for u in range(count):
            m = m0 + u
            t8 = pl.multiple_of(ltok_s[base + m], ch)
            y = xy_sc[pl.ds(m, ch, stride=XY_STRIDE), :]
            new.append((t8, acc_sc[pl.ds(t8, ch), :] + lw_s[base + m] * y))
        for t8, v in new:
            acc_sc[pl.ds(t8, ch), :] = v

    n_main = (n // SCATTER_UNROLL) * SCATTER_UNROLL
    lax.fori_loop(0, n // SCATTER_UNROLL, lambda g, c: (scatter(g * SCATTER_UNROLL, SCATTER_UNROLL), c)[1], 0)
    lax.fori_loop(n_main, n, lambda m, c: (scatter(m, 1), c)[1], 0)

    @pl.when(e == pl.num_programs(1) - 1)
    def _():
        def final_block(rb, carry):
            r0 = rb * EDGE_ROWS
            z = alpha * token_major_rows(h_ref, r0, EDGE_ROWS) + token_major_rows(acc_sc, r0, EDGE_ROWS)
            o_ref[pl.ds(pl.multiple_of(r0, EDGE_ROWS), EDGE_ROWS), :] = _layernorm(z, g_ref[...], b_ref[...])
            return carry

        lax.fori_loop(0, tt // EDGE_ROWS, final_block, 0)


def _moe(h_tm, code, wk, cnt, eg, eu, ed, sg, su, sd, ln_g, ln_b, alpha):
    E, D, F = eg.shape
    ch = D // LANES
    T = h_tm.shape[0] // ch
    assert ch == 8, "a token must fill exactly one 8-sublane group"
    tt = MOE_TILE
    full = lambda a: pl.BlockSpec(a.shape, lambda i, e: (0,) * a.ndim)
    return pl.pallas_call(
        functools.partial(_moe_kernel, alpha=alpha, ch=ch),
        grid=(T // tt, E),
        in_specs=[pl.BlockSpec((tt * ch, LANES), lambda i, e: (i, 0), pipeline_mode=pl.Buffered(1)),
                  pl.BlockSpec((1, tt // LANES, TOP_K, LANES), lambda i, e: (i, 0, 0, 0)),
                  pl.BlockSpec((1, tt // LANES, TOP_K, LANES), lambda i, e: (i, 0, 0, 0)),
                  pl.BlockSpec((1, 1, 128), lambda i, e: (i, 0, 0)),
                  pl.BlockSpec((1, D, F), lambda i, e: (e, 0, 0)),
                  pl.BlockSpec((1, D, F), lambda i, e: (e, 0, 0)),
                  pl.BlockSpec((1, F, D), lambda i, e: (e, 0, 0)),
                  full(sg), full(su), full(sd), full(ln_g), full(ln_b)],
        out_specs=pl.BlockSpec((tt, D), lambda i, e: (i, 0)),
        out_shape=jax.ShapeDtypeStruct((T, D), F32),
        scratch_shapes=[pltpu.VMEM((tt * ch, LANES), F32),
                        pltpu.VMEM((ch * XY_STRIDE, LANES), F32),
                        pltpu.VMEM((tt // LANES, TOP_K, LANES), jnp.int32),
                        pltpu.SMEM((tt // LANES, TOP_K, LANES), jnp.int32),
                        pltpu.SMEM((tt // LANES, TOP_K, LANES), F32),
                        pltpu.SMEM((1, 128), jnp.int32),
                        pltpu.SMEM((128,), jnp.int32),
                        pltpu.SMEM((TOP_K * tt,), jnp.int32),
                        pltpu.SMEM((TOP_K * tt,), F32),
                        pltpu.SemaphoreType.DMA((3,))],
        compiler_params=pltpu.CompilerParams(dimension_semantics=("arbitrary", "arbitrary"),
                                             vmem_limit_bytes=VMEM_LIMIT_BYTES),
        name="moe_experts",
    )(h_tm, code, wk, cnt, eg, eu, ed, sg, su, sd, ln_g, ln_b)


def kernel(x, w_in, swa_sinks, w_branch_swa, w_branch_moba, w_out, ln1_g, ln1_b, w_router, router_bias,
           w_exp_gate, w_exp_up, w_exp_down, w_sh_gate, w_sh_up, w_sh_down, ln2_g, ln2_b):
    B, S, D = x.shape
    depth = w_in.shape[0]
    alpha = (2.0 * depth) ** 0.25
    h = x.reshape(B * S, D)
    for l in range(depth):
        w_in_b = w_in[l].astype(BF16)
        qa, ka, va, qb, kb, vb, kmean = _qkv_proj(h, w_in_b[:, :QKV_COLS])
        att_a = _swa_attention(qa, ka, va, swa_sinks[l], B, S)
        att_b = _moba_attention(qb, kb, vb, kmean.reshape(B, S // MOBA_BLOCK, MOBA_W), B, S)
        h1, code, wk, cnt = _mix_ln_route(
            h, att_a, att_b, w_in_b[:, QKV_COLS:], w_branch_swa[l].astype(BF16), w_branch_moba[l].astype(BF16),
            w_out[l].astype(BF16), ln1_g[l].reshape(1, D), ln1_b[l].reshape(1, D),
            w_router[l].T, router_bias[l].reshape(N_EXPERTS, 1), alpha)
        h = _moe(h1, code, wk, cnt, w_exp_gate[l].astype(BF16), w_exp_up[l].astype(BF16),
                 w_exp_down[l].astype(BF16), w_sh_gate[l].astype(BF16), w_sh_up[l].astype(BF16),
                 w_sh_down[l].astype(BF16), ln2_g[l].reshape(1, D), ln2_b[l].reshape(1, D), alpha)
    return h.reshape(B, S, D)
```

```python
import functools

import numpy as np
import jax
import jax.numpy as jnp
from jax import lax
from jax.experimental import pallas as pl
from jax.experimental.pallas import tpu as pltpu

F32 = jnp.float32
BF16 = jnp.bfloat16

HEAD_DIM = 64
SWA_HEADS = 8
SWA_KV_HEADS = 2
SWA_BLOCK = 128
MOBA_HEADS = 8
MOBA_BLOCK = 256
MOBA_TOPK = 3
N_EXPERTS = 64
TOP_K = 8
N_GROUPS = 8
TOPK_GROUPS = 4
ROUTED_SCALE = 2.5
LN_EPS = 1e-5
NEG_INF = -1e30
REMOVED = -3e38

SWA_Q_W = SWA_HEADS * HEAD_DIM
SWA_KV_W = SWA_KV_HEADS * HEAD_DIM
MOBA_W = MOBA_HEADS * HEAD_DIM
QKV_COLS = SWA_Q_W + 2 * SWA_KV_W + 3 * MOBA_W

_N_SOFTMAX_HEADS = SWA_HEADS + MOBA_HEADS
_SLOPES = np.asarray(2.0 ** (-8.0 * np.arange(1, _N_SOFTMAX_HEADS + 1) / _N_SOFTMAX_HEADS), np.float32)
SWA_SLOPES = [float(s) for s in _SLOPES[:SWA_HEADS]]
MOBA_SLOPES = [float(s) for s in _SLOPES[SWA_HEADS:]]

LANES = 128
VMEM_LIMIT_BYTES = 56 * 1024 * 1024


def _dot(a, b):
    return jnp.dot(a, b, preferred_element_type=F32)


def _dot_nt(a, b, precision=None):
    return lax.dot_general(a, b, (((1,), (1,)), ((), ())), preferred_element_type=F32, precision=precision)


def _sigmoid(x):
    return 1.0 / (1.0 + jnp.exp(-x))


def _layernorm(z, g, b):
    mu = jnp.mean(z, axis=-1, keepdims=True)
    zc = z - mu
    var = jnp.mean(zc * zc, axis=-1, keepdims=True)
    return zc * lax.rsqrt(var + LN_EPS) * g + b


def _qkv_kernel(x_ref, w_ref, qa_ref, ka_ref, va_ref, qb_ref, kb_ref, vb_ref, km_ref, *, tm):
    xb = x_ref[...].astype(BF16)
    scale = HEAD_DIM ** -0.5

    def proj(lo, hi):
        return _dot(xb, w_ref[:, lo:hi])

    c = 0
    qa_ref[...] = (proj(c, c + SWA_Q_W) * scale).astype(BF16)
    c += SWA_Q_W
    ka_ref[...] = proj(c, c + SWA_KV_W).astype(BF16)
    c += SWA_KV_W
    va_ref[...] = proj(c, c + SWA_KV_W).astype(BF16)
    c += SWA_KV_W
    qb_ref[...] = (proj(c, c + MOBA_W) * scale).astype(BF16)
    c += MOBA_W
    kb = proj(c, c + MOBA_W)
    kb_ref[...] = kb.astype(BF16)
    for i in range(tm // MOBA_BLOCK):
        blk = kb[i * MOBA_BLOCK:(i + 1) * MOBA_BLOCK, :]
        km_ref[0, i:i + 1, :] = jnp.sum(blk, axis=0, keepdims=True) * (1.0 / MOBA_BLOCK)
    c += MOBA_W
    vb_ref[...] = proj(c, c + MOBA_W).astype(BF16)


def _qkv_proj(xf, w_qkv, tm=512):
    T, D = xf.shape
    nt = T // tm
    row = lambda w: pl.BlockSpec((tm, w), lambda i: (i, 0))
    outs = pl.pallas_call(
        functools.partial(_qkv_kernel, tm=tm),
        grid=(nt,),
        in_specs=[row(D), pl.BlockSpec((D, QKV_COLS), lambda i: (0, 0))],
        out_specs=[row(SWA_Q_W), row(SWA_KV_W), row(SWA_KV_W), row(MOBA_W), row(MOBA_W), row(MOBA_W),
                   pl.BlockSpec((1, tm // MOBA_BLOCK, MOBA_W), lambda i: (i, 0, 0))],
        out_shape=[jax.ShapeDtypeStruct((T, SWA_Q_W), BF16), jax.ShapeDtypeStruct((T, SWA_KV_W), BF16),
                   jax.ShapeDtypeStruct((T, SWA_KV_W), BF16), jax.ShapeDtypeStruct((T, MOBA_W), BF16),
                   jax.ShapeDtypeStruct((T, MOBA_W), BF16), jax.ShapeDtypeStruct((T, MOBA_W), BF16),
                   jax.ShapeDtypeStruct((nt, tm // MOBA_BLOCK, MOBA_W), F32)],
        compiler_params=pltpu.CompilerParams(dimension_semantics=("parallel",), vmem_limit_bytes=VMEM_LIMIT_BYTES),
        name="qkv_proj",
    )(xf, w_qkv)
    return outs


def _swa_kernel(sink_ref, q_ref, kp_ref, ko_ref, vp_ref, vo_ref, o_ref):
    n = pl.program_id(1)
    blk = SWA_BLOCK
    k = jnp.concatenate([kp_ref[...], ko_ref[...]], axis=0)
    v = jnp.concatenate([vp_ref[...], vo_ref[...]], axis=0)
    i = lax.broadcasted_iota(jnp.int32, (blk, 2 * blk), 0)
    j = lax.broadcasted_iota(jnp.int32, (blk, 2 * blk), 1)
    dist = i + blk - j
    valid = (dist >= 0) & (dist < blk) & ((j >= blk) | (n > 0))
    distf = dist.astype(F32)
    group = SWA_HEADS // SWA_KV_HEADS
    outs = []
    for h in range(SWA_HEADS):
        kv = h // group
        qh = q_ref[:, h * HEAD_DIM:(h + 1) * HEAD_DIM]
        kh = k[:, kv * HEAD_DIM:(kv + 1) * HEAD_DIM]
        vh = v[:, kv * HEAD_DIM:(kv + 1) * HEAD_DIM]
        s = _dot_nt(qh, kh)
        s = jnp.where(valid, s - SWA_SLOPES[h] * distf, NEG_INF)
        sink = sink_ref[h]
        m = jnp.maximum(jnp.max(s, axis=-1, keepdims=True), sink)
        e = jnp.exp(s - m)
        denom = jnp.sum(e, axis=-1, keepdims=True) + jnp.exp(sink - m)
        outs.append(_dot(e.astype(BF16), vh) / denom)
    o_ref[...] = jnp.concatenate(outs, axis=-1).astype(BF16)


def _swa_attention(qa, ka, va, sinks, B, S):
    nq = S // SWA_BLOCK
    own = lambda b, n: (b * nq + n, 0)
    prev = lambda b, n: (b * nq + jnp.maximum(n - 1, 0), 0)
    kv_spec = lambda im: pl.BlockSpec((SWA_BLOCK, SWA_KV_W), im)
    return pl.pallas_call(
        _swa_kernel,
        grid=(B, nq),
        in_specs=[pl.BlockSpec(memory_space=pltpu.SMEM),
                  pl.BlockSpec((SWA_BLOCK, SWA_Q_W), own),
                  kv_spec(prev), kv_spec(own), kv_spec(prev), kv_spec(own)],
        out_specs=pl.BlockSpec((SWA_BLOCK, SWA_Q_W), own),
        out_shape=jax.ShapeDtypeStruct((B * S, SWA_Q_W), BF16),
        compiler_params=pltpu.CompilerParams(dimension_semantics=("parallel", "parallel")),
        name="swa_attention",
    )(sinks, qa, ka, ka, va, va)


MOBA_ROWS = 256


def _moba_bias_table(S, nb):
    pos = np.arange(S)
    onehot = np.zeros((S, MOBA_HEADS, HEAD_DIM), np.float32)
    onehot[pos, :, pos // MOBA_BLOCK] = 1.0
    tab = jnp.asarray(onehot)
    rem = jnp.asarray(np.asarray(MOBA_SLOPES, np.float32)[None, :] * pos[:, None].astype(np.float32))
    for t in range(3):
        part = rem.astype(BF16).astype(F32)
        tab = tab.at[:, :, nb + t].set(part)
        rem = rem - part
    return tab.reshape(S, MOBA_HEADS * HEAD_DIM).astype(BF16)


def _moba_kernel(q_ref, k_ref, v_ref, km_ref, tab_ref, o_ref, kaug_sc, s_sc, *, nb):
    qi = pl.program_id(2)
    blk = MOBA_BLOCK
    hd = HEAD_DIM
    rows_per = MOBA_ROWS

    @pl.when(qi == 0)
    def _():
        for hh in range(2):
            kaug_sc[:, 2 * hh * hd:(2 * hh + 1) * hd] = k_ref[:, hh * hd:(hh + 1) * hd]
            kaug_sc[:, (2 * hh + 1) * hd:(2 * hh + 2) * hd] = tab_ref[:, hh * hd:(hh + 1) * hd]

    ri = lax.broadcasted_iota(jnp.int32, (rows_per, blk), 0)
    cj = lax.broadcasted_iota(jnp.int32, (rows_per, blk), 1)
    n_iota = lax.broadcasted_iota(jnp.int32, (nb, blk), 0)
    lane64 = lax.broadcasted_iota(jnp.int32, (1, hd), 1)
    ones_row = jnp.where((lane64 >= nb) & (lane64 < nb + 3), 1.0, 0.0)
    out_lane = lax.broadcasted_iota(jnp.int32, (rows_per, 2 * hd), 1)

    def tile(c):
        q_augs = []
        for hh in range(2):
            lanes = slice(hh * hd, (hh + 1) * hd)
            qh = q_ref[:, lanes]
            g_t = _dot_nt(km_ref[0, :, lanes], qh.astype(F32), precision=lax.Precision.HIGHEST)
            rank = jnp.zeros((nb, blk), jnp.int32)
            for mblk in range(c):
                gm = g_t[mblk:mblk + 1, :]
                beats = (gm > g_t) | ((gm == g_t) & (mblk < n_iota))
                rank = rank + jnp.where(beats, 1, 0)
            keep = ((n_iota < c) & (rank < MOBA_TOPK)) | (n_iota == c)
            selb = jnp.where(keep, 0.0, NEG_INF)
            selb_t = jnp.concatenate([selb, jnp.zeros((128 - nb, blk), F32)], axis=0).T
            q_bias = (selb_t[:, :hd] + ones_row).astype(BF16)
            q_augs.append(jnp.concatenate([qh, q_bias], axis=1))
        res = []
        for hh in range(2):
            q_aug = q_augs[hh]
            per_chunk = []
            for rc in range(blk // rows_per):
                qa = q_aug[rc * rows_per:(rc + 1) * rows_per]
                mx = None
                for n in range(c + 1):
                    s = _dot_nt(qa, kaug_sc[n * blk:(n + 1) * blk, 2 * hh * hd:(2 * hh + 2) * hd])
                    if n == c:
                        s = jnp.where(ri + rc * rows_per >= cj, s, NEG_INF)
                    s_sc[hh, rc, n] = s
                    t = jnp.maximum(s[:, :128], s[:, 128:])
                    mx = t if mx is None else jnp.maximum(mx, t)
                m = jnp.max(mx, axis=-1, keepdims=True)
                lsum = None
                acc = None
                for n in range(c + 1):
                    p = jnp.exp(s_sc[hh, rc, n] - m)
                    t = p[:, :128] + p[:, 128:]
                    lsum = t if lsum is None else lsum + t
                    pv = _dot(p.astype(BF16), v_ref[n * blk:(n + 1) * blk, :])
                    acc = pv if acc is None else acc + pv
                per_chunk.append(acc / jnp.sum(lsum, axis=-1, keepdims=True))
            res.append(per_chunk)
        for rc in range(blk // rows_per):
            o = jnp.where(out_lane < hd, res[0][rc], res[1][rc])
            o_ref[rc * rows_per:(rc + 1) * rows_per, :] = o.astype(BF16)

    for c in range(nb):
        pl.when(qi == c)(functools.partial(tile, c))


def _moba_attention(qb, kb, vb, kmean, B, S):
    nb = S // MOBA_BLOCK
    blk = MOBA_BLOCK
    hpairs = MOBA_HEADS // 2
    pair = 2 * HEAD_DIM
    tab = _moba_bias_table(S, nb)
    return pl.pallas_call(
        functools.partial(_moba_kernel, nb=nb),
        grid=(B, hpairs, nb),
        in_specs=[pl.BlockSpec((blk, pair), lambda b, h, q: (b * nb + q, h)),
                  pl.BlockSpec((S, pair), lambda b, h, q: (b, h)),
                  pl.BlockSpec((S, pair), lambda b, h, q: (b, h)),
                  pl.BlockSpec((1, nb, pair), lambda b, h, q: (b, 0, h)),
                  pl.BlockSpec((S, pair), lambda b, h, q: (0, h))],
        out_specs=pl.BlockSpec((blk, pair), lambda b, h, q: (b * nb + q, h)),
        out_shape=jax.ShapeDtypeStruct((B * S, MOBA_W), BF16),
        scratch_shapes=[pltpu.VMEM((S, 2 * pair), BF16),
                        pltpu.VMEM((2, blk // MOBA_ROWS, nb, MOBA_ROWS, blk), F32)],
        compiler_params=pltpu.CompilerParams(dimension_semantics=("parallel", "parallel", "arbitrary"),
                                             vmem_limit_bytes=VMEM_LIMIT_BYTES),
        name="moba_attention",
    )(qb, kb, vb, kmean, tab)


def _route(scores_t, biased, tm):
    gsz = N_EXPERTS // N_GROUPS
    sub = lax.broadcasted_iota(jnp.int32, (gsz, tm), 0)
    grp = [biased[g * gsz:(g + 1) * gsz, :] for g in range(N_GROUPS)]
    gscore = []
    for v in grp:
        m1 = jnp.max(v, axis=0, keepdims=True)
        first = jnp.min(jnp.where(v == m1, sub, gsz), axis=0, keepdims=True)
        m2 = jnp.max(jnp.where(sub == first, REMOVED, v), axis=0, keepdims=True)
        gscore.append(m1 + m2)
    masked = []
    for g in range(N_GROUPS):
        rank = jnp.zeros((1, tm), jnp.int32)
        for g2 in range(N_GROUPS):
            if g2 == g:
                continue
            beats = (gscore[g2] >= gscore[g]) if g2 < g else (gscore[g2] > gscore[g])
            rank = rank + jnp.where(beats, 1, 0)
        masked.append(jnp.where(rank < TOPK_GROUPS, grp[g], NEG_INF))
    v = jnp.concatenate(masked, axis=0)
    e_iota = lax.broadcasted_iota(jnp.int32, (N_EXPERTS, tm), 0)
    picked = jnp.zeros((N_EXPERTS, tm), F32)
    experts, svals = [], []
    for _ in range(TOP_K):
        m = jnp.max(v, axis=0, keepdims=True)
        first = jnp.min(jnp.where(v == m, e_iota, N_EXPERTS), axis=0, keepdims=True)
        pick = e_iota == first
        experts.append(first)
        svals.append(jnp.sum(jnp.where(pick, scores_t, 0.0), axis=0, keepdims=True))
        picked = jnp.where(pick, 1.0, picked)
        v = jnp.where(pick, REMOVED, v)
    denom = svals[0]
    for s in svals[1:]:
        denom = denom + s
    weights = [s / denom * ROUTED_SCALE for s in svals]
    return experts, weights, picked


MOE_TILE = 2048
ROUTE_SHIFT = 12
assert MOE_TILE <= (1 << ROUTE_SHIFT)


def _mix_kernel(x_ref, aa_ref, ab_ref, wg_ref, wa_ref, wb_ref, wo_ref, g_ref, b_ref, wr_ref, rb_ref, tri_ref,
                h_ref, code_ref, wk_ref, cnt_ref, run_sc, *, alpha, tm):
    step = pl.program_id(0) % (MOE_TILE // tm)
    x = x_ref[...]
    xb = x.astype(BF16)
    d = x.shape[-1]
    ga = _dot(xb, wg_ref[:, :d])
    gb = _dot(xb, wg_ref[:, d:])
    y = _sigmoid(ga) * _dot(aa_ref[...], wa_ref[...]) + _sigmoid(gb) * _dot(ab_ref[...], wb_ref[...])
    mix = _dot(y.astype(BF16), wo_ref[...])
    h = _layernorm(alpha * x + mix, g_ref[...], b_ref[...])
    for j in range(d // LANES):
        h_ref[pl.ds(j, tm, stride=d // LANES), :] = h[:, j * LANES:(j + 1) * LANES]
    logits_t = _dot_nt(wr_ref[...], h, precision=lax.Precision.HIGHEST)
    scores_t = _sigmoid(logits_t)
    experts, weights, picked = _route(scores_t, scores_t + rb_ref[...], tm)

    @pl.when(step == 0)
    def _():
        run_sc[...] = jnp.zeros_like(run_sc)

    pos = _dot(picked.astype(BF16), tri_ref[...]) + run_sc[...]
    run_sc[...] += jnp.sum(picked, axis=1, keepdims=True)
    e_iota = lax.broadcasted_iota(jnp.int32, (N_EXPERTS, tm), 0)
    codes = []
    for k in range(TOP_K):
        rank = jnp.sum(jnp.where(e_iota == experts[k], pos, 0.0), axis=0, keepdims=True)
        codes.append(experts[k] * (1 << ROUTE_SHIFT) + rank.astype(jnp.int32))
    for c in range(tm // LANES):
        lanes = slice(c * LANES, (c + 1) * LANES)
        code_ref[0, c] = jnp.concatenate([v[:, lanes] for v in codes], axis=0)
        wk_ref[0, c] = jnp.concatenate([v[:, lanes] for v in weights], axis=0)

    picked_t = jnp.concatenate([picked, jnp.zeros((128 - N_EXPERTS, tm), F32)], axis=0).T
    counts = jnp.sum(picked_t, axis=0, keepdims=True).astype(jnp.int32)

    @pl.when(step == 0)
    def _():
        cnt_ref[0] = counts

    @pl.when(step != 0)
    def _():
        cnt_ref[0] += counts


def _mix_ln_route(xf, att_a, att_b, w_gates, w_a, w_b, w_o, ln_g, ln_b, w_router_t, router_bias, alpha, tm=512):
    T, D = xf.shape
    assert MOE_TILE % tm == 0 and T % MOE_TILE == 0
    steps = MOE_TILE // tm
    tri = jnp.asarray(np.triu(np.ones((tm, tm), np.float32), k=1), BF16)
    row = lambda w: pl.BlockSpec((tm, w), lambda i: (i, 0))
    col = pl.BlockSpec((1, tm // LANES, TOP_K, LANES), lambda i: (i // steps, i % steps, 0, 0))
    rec_shape = (T // MOE_TILE, MOE_TILE // LANES, TOP_K, LANES)
    full = lambda a: pl.BlockSpec(a.shape, lambda i: (0,) * a.ndim)
    return pl.pallas_call(
        functools.partial(_mix_kernel, alpha=alpha, tm=tm),
        grid=(T // tm,),
        in_specs=[row(D), row(SWA_Q_W), row(MOBA_W), full(w_gates), full(w_a), full(w_b), full(w_o),
                  full(ln_g), full(ln_b), full(w_router_t), full(router_bias), full(tri)],
        out_specs=[pl.BlockSpec((tm * (D // LANES), LANES), lambda i: (i, 0)), col, col, pl.BlockSpec((1, 1, 128), lambda i: (i // steps, 0, 0))],
        out_shape=[jax.ShapeDtypeStruct((T * (D // LANES), LANES), F32), jax.ShapeDtypeStruct(rec_shape, jnp.int32),
                   jax.ShapeDtypeStruct(rec_shape, F32), jax.ShapeDtypeStruct((T // MOE_TILE, 1, 128), jnp.int32)],
        scratch_shapes=[pltpu.VMEM((N_EXPERTS, 1), F32)],
        compiler_params=pltpu.CompilerParams(dimension_semantics=("arbitrary",), vmem_limit_bytes=VMEM_LIMIT_BYTES),
        name="mix_ln_route",
    )(xf, att_a, att_b, w_gates, w_a, w_b, w_o, ln_g, ln_b, w_router_t, router_bias, tri)


XY_STRIDE = MOE_TILE + 8
MOE_ROWS = 64
MOE_STATIC_BLOCKS = 12
EDGE_ROWS = 256
GATHER_UNROLL = 8
SCATTER_UNROLL = 8


def _moe_kernel(h_ref, code_ref, wk_ref, cnt_ref, eg_ref, eu_ref, ed_ref, sg_ref, su_ref, sd_ref, g_ref, b_ref,
                o_ref, acc_sc, xy_sc, dst_sc, dst_s, wk_s, cnt_s, off_s, pick_s, sem, *, alpha, ch):
    tile = pl.program_id(0)
    e = pl.program_id(1)
    tt = MOE_TILE

    def token_major_rows(ref, r0, rows):
        start = pl.multiple_of(r0 * ch, ch)
        return jnp.concatenate([ref[pl.ds(start + j, rows, stride=ch), :] for j in range(ch)], axis=1)

    def swiglu(xb, wg, wu, wd):
        a = _dot(xb, wg)
        return _dot((a * _sigmoid(a) * _dot(xb, wu)).astype(BF16), wd)

    @pl.when((tile == 0) & (e == 0))
    def _():
        xy_sc[...] = jnp.zeros_like(xy_sc)

    @pl.when(e == 0)
    def _():
        copies = [pltpu.make_async_copy(wk_ref.at[tile], wk_s, sem.at[1]),
                  pltpu.make_async_copy(cnt_ref.at[0], cnt_s, sem.at[2])]
        for c in copies:
            c.start()

        def shared_block(rb, carry):
            r0 = rb * EDGE_ROWS
            y = swiglu(token_major_rows(h_ref, r0, EDGE_ROWS).astype(BF16), sg_ref[...], su_ref[...], sd_ref[...])
            start = pl.multiple_of(r0 * ch, ch)
            for j in range(ch):
                acc_sc[pl.ds(start + j, EDGE_ROWS, stride=ch), :] = y[:, j * LANES:(j + 1) * LANES]
            return carry

        lax.fori_loop(0, tt // EDGE_ROWS, shared_block, 0)
        for c in copies:
            c.wait()

        total = jnp.int32(0)
        for x in range(N_EXPERTS):
            off_s[x] = total
            total = total + cnt_s[0, x]
        off_s[N_EXPERTS] = total
        code = code_ref[0]
        expert = code >> ROUTE_SHIFT
        dst = code & ((1 << ROUTE_SHIFT) - 1)
        for x in range(1, N_EXPERTS):
            dst = dst + jnp.where(expert == x, off_s[x], 0)
        dst_sc[...] = dst
        dst_copy = pltpu.make_async_copy(dst_sc, dst_s, sem.at[0])
        dst_copy.start()
        dst_copy.wait()

        def build(r, carry):
            for u in range(LANES):
                pick_s[dst_s[r, u]] = r * LANES + u
            return carry

        lax.fori_loop(0, tt * TOP_K // LANES, build, 0)

    base = off_s[e]
    n = off_s[e + 1] - base

    def token_rows(p):
        return pl.multiple_of((p >> 3) << 3, ch)

    def gather(m0, count):
        for u in range(count):
            m = m0 + u
            xy_sc[pl.ds(m, ch, stride=XY_STRIDE), :] = h_ref[pl.ds(token_rows(pick_s[base + m]), ch), :]

    n_main = (n // GATHER_UNROLL) * GATHER_UNROLL
    lax.fori_loop(0, n // GATHER_UNROLL, lambda g, c: (gather(g * GATHER_UNROLL, GATHER_UNROLL), c)[1], 0)
    lax.fori_loop(n_main, n, lambda m, c: (gather(m, 1), c)[1], 0)

    def expert_rows(r0, rows):
        xb = jnp.concatenate([xy_sc[pl.ds(j * XY_STRIDE + r0, rows), :] for j in range(ch)], axis=1).astype(BF16)
        y = swiglu(xb, eg_ref[0], eu_ref[0], ed_ref[0])
        for j in range(ch):
            xy_sc[pl.ds(j * XY_STRIDE + r0, rows), :] = y[:, j * LANES:(j + 1) * LANES]

    n_blocks = (n + MOE_ROWS - 1) // MOE_ROWS
    for v in range(1, MOE_STATIC_BLOCKS + 1):
        pl.when(n_blocks == v)(functools.partial(expert_rows, 0, v * MOE_ROWS))

    @pl.when(n_blocks > MOE_STATIC_BLOCKS)
    def _():
        def expert_block(rb, carry):
            expert_rows(pl.multiple_of(rb * MOE_ROWS, MOE_ROWS), MOE_ROWS)
            return carry

        lax.fori_loop(0, n_blocks, expert_block, 0)

    def scatter(m0, count):
        new = []
        for u in range(count):
            m = m0 + u
            p = pick_s[base + m]
            t8 = token_rows(p)
            y = xy_sc[pl.ds(m, ch, stride=XY_STRIDE), :]
            new.append((t8, acc_sc[pl.ds(t8, ch), :] + wk_s[p] * y))
        for t8, v in new:
            acc_sc[pl.ds(t8, ch), :] = v

    n_main = (n // SCATTER_UNROLL) * SCATTER_UNROLL
    lax.fori_loop(0, n // SCATTER_UNROLL, lambda g, c: (scatter(g * SCATTER_UNROLL, SCATTER_UNROLL), c)[1], 0)
    lax.fori_loop(n_main, n, lambda m, c: (scatter(m, 1), c)[1], 0)

    @pl.when(e == pl.num_programs(1) - 1)
    def _():
        def final_block(rb, carry):
            r0 = rb * EDGE_ROWS
            z = alpha * token_major_rows(h_ref, r0, EDGE_ROWS) + token_major_rows(acc_sc, r0, EDGE_ROWS)
            o_ref[pl.ds(pl.multiple_of(r0, EDGE_ROWS), EDGE_ROWS), :] = _layernorm(z, g_ref[...], b_ref[...])
            return carry

        lax.fori_loop(0, tt // EDGE_ROWS, final_block, 0)


def _moe(h_tm, code, wk, cnt, eg, eu, ed, sg, su, sd, ln_g, ln_b, alpha):
    E, D, F = eg.shape
    ch = D // LANES
    T = h_tm.shape[0] // ch
    assert ch == 8 and TOP_K == 8, "a token fills one 8-sublane group; pick ids are token * 8 + round"
    tt = MOE_TILE
    nt = T // tt
    rec_rows = tt * TOP_K // LANES
    code = code.transpose(0, 1, 3, 2).reshape(nt, rec_rows, LANES)
    wk = wk.transpose(0, 1, 3, 2).reshape(nt, tt * TOP_K)
    full = lambda a: pl.BlockSpec(a.shape, lambda i, e: (0,) * a.ndim)
    return pl.pallas_call(
        functools.partial(_moe_kernel, alpha=alpha, ch=ch),
        grid=(T // tt, E),
        in_specs=[pl.BlockSpec((tt * ch, LANES), lambda i, e: (i, 0), pipeline_mode=pl.Buffered(1)),
                  pl.BlockSpec((1, rec_rows, LANES), lambda i, e: (i, 0, 0)),
                  pl.BlockSpec(memory_space=pl.ANY),
                  pl.BlockSpec((1, 1, 128), lambda i, e: (i, 0, 0)),
                  pl.BlockSpec((1, D, F), lambda i, e: (e, 0, 0)),
                  pl.BlockSpec((1, D, F), lambda i, e: (e, 0, 0)),
                  pl.BlockSpec((1, F, D), lambda i, e: (e, 0, 0)),
                  full(sg), full(su), full(sd), full(ln_g), full(ln_b)],
        out_specs=pl.BlockSpec((tt, D), lambda i, e: (i, 0)),
        out_shape=jax.ShapeDtypeStruct((T, D), F32),
        scratch_shapes=[pltpu.VMEM((tt * ch, LANES), F32),
                        pltpu.VMEM((ch * XY_STRIDE, LANES), F32),
                        pltpu.VMEM((rec_rows, LANES), jnp.int32),
                        pltpu.SMEM((rec_rows, LANES), jnp.int32),
                        pltpu.SMEM((TOP_K * tt,), F32),
                        pltpu.SMEM((1, 128), jnp.int32),
                        pltpu.SMEM((128,), jnp.int32),
                        pltpu.SMEM((TOP_K * tt,), jnp.int32),
                        pltpu.SemaphoreType.DMA((3,))],
        compiler_params=pltpu.CompilerParams(dimension_semantics=("arbitrary", "arbitrary"),
                                             vmem_limit_bytes=VMEM_LIMIT_BYTES),
        name="moe_experts",
    )(h_tm, code, wk, cnt, eg, eu, ed, sg, su, sd, ln_g, ln_b)


def kernel(x, w_in, swa_sinks, w_branch_swa, w_branch_moba, w_out, ln1_g, ln1_b, w_router, router_bias,
           w_exp_gate, w_exp_up, w_exp_down, w_sh_gate, w_sh_up, w_sh_down, ln2_g, ln2_b):
    B, S, D = x.shape
    depth = w_in.shape[0]
    alpha = (2.0 * depth) ** 0.25
    h = x.reshape(B * S, D)
    for l in range(depth):
        w_in_b = w_in[l].astype(BF16)
        qa, ka, va, qb, kb, vb, kmean = _qkv_proj(h, w_in_b[:, :QKV_COLS])
        att_a = _swa_attention(qa, ka, va, swa_sinks[l], B, S)
        att_b = _moba_attention(qb, kb, vb, kmean.reshape(B, S // MOBA_BLOCK, MOBA_W), B, S)
        h1, code, wk, cnt = _mix_ln_route(
            h, att_a, att_b, w_in_b[:, QKV_COLS:], w_branch_swa[l].astype(BF16), w_branch_moba[l].astype(BF16),
            w_out[l].astype(BF16), ln1_g[l].reshape(1, D), ln1_b[l].reshape(1, D),
            w_router[l].T, router_bias[l].reshape(N_EXPERTS, 1), alpha)
        h = _moe(h1, code, wk, cnt, w_exp_gate[l].astype(BF16), w_exp_up[l].astype(BF16),
                 w_exp_down[l].astype(BF16), w_sh_gate[l].astype(BF16), w_sh_up[l].astype(BF16),
                 w_sh_down[l].astype(BF16), ln2_g[l].reshape(1, D), ln2_b[l].reshape(1, D), alpha)
    return h.reshape(B, S, D)
```

```python
import functools

import numpy as np
import jax
import jax.numpy as jnp
from jax import lax
from jax.experimental import pallas as pl
from jax.experimental.pallas import tpu as pltpu

F32 = jnp.float32
BF16 = jnp.bfloat16

HEAD_DIM = 64
SWA_HEADS = 8
SWA_KV_HEADS = 2
SWA_BLOCK = 128
MOBA_HEADS = 8
MOBA_BLOCK = 256
MOBA_TOPK = 3
N_EXPERTS = 64
TOP_K = 8
N_GROUPS = 8
TOPK_GROUPS = 4
ROUTED_SCALE = 2.5
LN_EPS = 1e-5
NEG_INF = -1e30
REMOVED = -3e38

SWA_Q_W = SWA_HEADS * HEAD_DIM
SWA_KV_W = SWA_KV_HEADS * HEAD_DIM
MOBA_W = MOBA_HEADS * HEAD_DIM
QKV_COLS = SWA_Q_W + 2 * SWA_KV_W + 3 * MOBA_W

_N_SOFTMAX_HEADS = SWA_HEADS + MOBA_HEADS
_SLOPES = np.asarray(2.0 ** (-8.0 * np.arange(1, _N_SOFTMAX_HEADS + 1) / _N_SOFTMAX_HEADS), np.float32)
SWA_SLOPES = [float(s) for s in _SLOPES[:SWA_HEADS]]
MOBA_SLOPES = [float(s) for s in _SLOPES[SWA_HEADS:]]

LANES = 128
VMEM_LIMIT_BYTES = 56 * 1024 * 1024


def _dot(a, b):
    return jnp.dot(a, b, preferred_element_type=F32)


def _dot_nt(a, b, precision=None):
    return lax.dot_general(a, b, (((1,), (1,)), ((), ())), preferred_element_type=F32, precision=precision)


def _sigmoid(x):
    return 1.0 / (1.0 + jnp.exp(-x))


def _layernorm(z, g, b):
    mu = jnp.mean(z, axis=-1, keepdims=True)
    zc = z - mu
    var = jnp.mean(zc * zc, axis=-1, keepdims=True)
    return zc * lax.rsqrt(var + LN_EPS) * g + b


def _qkv_kernel(x_ref, w_ref, qa_ref, ka_ref, va_ref, qb_ref, kb_ref, vb_ref, km_ref, *, tm):
    xb = x_ref[...].astype(BF16)
    scale = HEAD_DIM ** -0.5

    def proj(lo, hi):
        return _dot(xb, w_ref[:, lo:hi])

    c = 0
    qa_ref[...] = (proj(c, c + SWA_Q_W) * scale).astype(BF16)
    c += SWA_Q_W
    ka_ref[...] = proj(c, c + SWA_KV_W).astype(BF16)
    c += SWA_KV_W
    va_ref[...] = proj(c, c + SWA_KV_W).astype(BF16)
    c += SWA_KV_W
    qb_ref[...] = (proj(c, c + MOBA_W) * scale).astype(BF16)
    c += MOBA_W
    kb = proj(c, c + MOBA_W)
    kb_ref[...] = kb.astype(BF16)
    for i in range(tm // MOBA_BLOCK):
        blk = kb[i * MOBA_BLOCK:(i + 1) * MOBA_BLOCK, :]
        km_ref[0, i:i + 1, :] = jnp.sum(blk, axis=0, keepdims=True) * (1.0 / MOBA_BLOCK)
    c += MOBA_W
    vb_ref[...] = proj(c, c + MOBA_W).astype(BF16)


def _qkv_proj(xf, w_qkv, tm=512):
    T, D = xf.shape
    nt = T // tm
    row = lambda w: pl.BlockSpec((tm, w), lambda i: (i, 0))
    outs = pl.pallas_call(
        functools.partial(_qkv_kernel, tm=tm),
        grid=(nt,),
        in_specs=[row(D), pl.BlockSpec((D, QKV_COLS), lambda i: (0, 0))],
        out_specs=[row(SWA_Q_W), row(SWA_KV_W), row(SWA_KV_W), row(MOBA_W), row(MOBA_W), row(MOBA_W),
                   pl.BlockSpec((1, tm // MOBA_BLOCK, MOBA_W), lambda i: (i, 0, 0))],
        out_shape=[jax.ShapeDtypeStruct((T, SWA_Q_W), BF16), jax.ShapeDtypeStruct((T, SWA_KV_W), BF16),
                   jax.ShapeDtypeStruct((T, SWA_KV_W), BF16), jax.ShapeDtypeStruct((T, MOBA_W), BF16),
                   jax.ShapeDtypeStruct((T, MOBA_W), BF16), jax.ShapeDtypeStruct((T, MOBA_W), BF16),
                   jax.ShapeDtypeStruct((nt, tm // MOBA_BLOCK, MOBA_W), F32)],
        compiler_params=pltpu.CompilerParams(dimension_semantics=("parallel",), vmem_limit_bytes=VMEM_LIMIT_BYTES),
        name="qkv_proj",
    )(xf, w_qkv)
    return outs


def _swa_bias_table():
    blk = SWA_BLOCK
    i = np.arange(blk)[:, None]
    j = np.arange(2 * blk)[None, :]
    dist = i + blk - j
    band = (dist >= 0) & (dist < blk)
    alibi = -(np.asarray(SWA_SLOPES, np.float32)[:, None, None] * dist.astype(np.float32)[None])
    tabs = [np.where((band & (j >= blk))[None], alibi, np.float32(NEG_INF)),
            np.where(band[None], alibi, np.float32(NEG_INF))]
    return jnp.asarray(np.stack(tabs).astype(np.float32))


def _swa_kernel(sink_ref, q_ref, kp_ref, ko_ref, vp_ref, vo_ref, bias_ref, o_ref):
    blk = SWA_BLOCK
    k = jnp.concatenate([kp_ref[...], ko_ref[...]], axis=0)
    v = jnp.concatenate([vp_ref[...], vo_ref[...]], axis=0)
    group = SWA_HEADS // SWA_KV_HEADS
    outs = []
    for h in range(SWA_HEADS):
        kv = h // group
        s = _dot_nt(q_ref[:, h * HEAD_DIM:(h + 1) * HEAD_DIM], k[:, kv * HEAD_DIM:(kv + 1) * HEAD_DIM]) + bias_ref[0, h]
        sink = sink_ref[h]
        m = jnp.maximum(jnp.max(jnp.maximum(s[:, :blk], s[:, blk:]), axis=-1, keepdims=True), sink)
        e = jnp.exp(s - m)
        denom = jnp.sum(e[:, :blk] + e[:, blk:], axis=-1, keepdims=True) + jnp.exp(sink - m)
        outs.append(_dot(e.astype(BF16), v[:, kv * HEAD_DIM:(kv + 1) * HEAD_DIM]) / denom)
    o_ref[...] = jnp.concatenate(outs, axis=-1).astype(BF16)


def _swa_attention(qa, ka, va, sinks, B, S):
    nq = S // SWA_BLOCK
    own = lambda b, n: (b * nq + n, 0)
    prev = lambda b, n: (b * nq + jnp.maximum(n - 1, 0), 0)
    kv_spec = lambda im: pl.BlockSpec((SWA_BLOCK, SWA_KV_W), im)
    return pl.pallas_call(
        _swa_kernel,
        grid=(B, nq),
        in_specs=[pl.BlockSpec(memory_space=pltpu.SMEM),
                  pl.BlockSpec((SWA_BLOCK, SWA_Q_W), own),
                  kv_spec(prev), kv_spec(own), kv_spec(prev), kv_spec(own),
                  pl.BlockSpec((1, SWA_HEADS, SWA_BLOCK, 2 * SWA_BLOCK), lambda b, n: (jnp.minimum(n, 1), 0, 0, 0))],
        out_specs=pl.BlockSpec((SWA_BLOCK, SWA_Q_W), own),
        out_shape=jax.ShapeDtypeStruct((B * S, SWA_Q_W), BF16),
        compiler_params=pltpu.CompilerParams(dimension_semantics=("parallel", "parallel")),
        name="swa_attention",
    )(sinks, qa, ka, ka, va, va, _swa_bias_table())


MOBA_ROWS = 256


def _moba_bias_table(S, nb):
    pos = np.arange(S)
    onehot = np.zeros((S, MOBA_HEADS, HEAD_DIM), np.float32)
    onehot[pos, :, pos // MOBA_BLOCK] = 1.0
    tab = jnp.asarray(onehot)
    rem = jnp.asarray(np.asarray(MOBA_SLOPES, np.float32)[None, :] * pos[:, None].astype(np.float32))
    for t in range(3):
        part = rem.astype(BF16).astype(F32)
        tab = tab.at[:, :, nb + t].set(part)
        rem = rem - part
    return tab.reshape(S, MOBA_HEADS * HEAD_DIM).astype(BF16)


def _moba_kernel(q_ref, k_ref, v_ref, km_ref, tab_ref, o_ref, kaug_sc, s_sc, *, nb):
    qi = pl.program_id(2)
    blk = MOBA_BLOCK
    hd = HEAD_DIM
    rows_per = MOBA_ROWS

    @pl.when(qi == 0)
    def _():
        for hh in range(2):
            kaug_sc[:, 2 * hh * hd:(2 * hh + 1) * hd] = k_ref[:, hh * hd:(hh + 1) * hd]
            kaug_sc[:, (2 * hh + 1) * hd:(2 * hh + 2) * hd] = tab_ref[:, hh * hd:(hh + 1) * hd]

    ri = lax.broadcasted_iota(jnp.int32, (rows_per, blk), 0)
    cj = lax.broadcasted_iota(jnp.int32, (rows_per, blk), 1)
    n_iota = lax.broadcasted_iota(jnp.int32, (nb, blk), 0)
    lane64 = lax.broadcasted_iota(jnp.int32, (1, hd), 1)
    ones_row = jnp.where((lane64 >= nb) & (lane64 < nb + 3), 1.0, 0.0)
    out_lane = lax.broadcasted_iota(jnp.int32, (rows_per, 2 * hd), 1)

    def tile(c):
        q_augs = []
        for hh in range(2):
            lanes = slice(hh * hd, (hh + 1) * hd)
            qh = q_ref[:, lanes]
            g_t = _dot_nt(km_ref[0, :, lanes], qh.astype(F32), precision=lax.Precision.HIGHEST)
            rank = jnp.zeros((nb, blk), jnp.int32)
            for mblk in range(c):
                gm = g_t[mblk:mblk + 1, :]
                beats = (gm > g_t) | ((gm == g_t) & (mblk < n_iota))
                rank = rank + jnp.where(beats, 1, 0)
            keep = ((n_iota < c) & (rank < MOBA_TOPK)) | (n_iota == c)
            selb = jnp.where(keep, 0.0, NEG_INF)
            selb_t = jnp.concatenate([selb, jnp.zeros((128 - nb, blk), F32)], axis=0).T
            q_bias = (selb_t[:, :hd] + ones_row).astype(BF16)
            q_augs.append(jnp.concatenate([qh, q_bias], axis=1))
        res = []
        for hh in range(2):
            q_aug = q_augs[hh]
            per_chunk = []
            for rc in range(blk // rows_per):
                qa = q_aug[rc * rows_per:(rc + 1) * rows_per]
                mx = None
                for n in range(c + 1):
                    s = _dot_nt(qa, kaug_sc[n * blk:(n + 1) * blk, 2 * hh * hd:(2 * hh + 2) * hd])
                    if n == c:
                        s = jnp.where(ri + rc * rows_per >= cj, s, NEG_INF)
                    s_sc[hh, rc, n] = s
                    t = jnp.maximum(s[:, :128], s[:, 128:])
                    mx = t if mx is None else jnp.maximum(mx, t)
                m = jnp.max(mx, axis=-1, keepdims=True)
                lsum = None
                acc = None
                for n in range(c + 1):
                    p = jnp.exp(s_sc[hh, rc, n] - m)
                    t = p[:, :128] + p[:, 128:]
                    lsum = t if lsum is None else lsum + t
                    pv = _dot(p.astype(BF16), v_ref[n * blk:(n + 1) * blk, :])
                    acc = pv if acc is None else acc + pv
                per_chunk.append(acc / jnp.sum(lsum, axis=-1, keepdims=True))
            res.append(per_chunk)
        for rc in range(blk // rows_per):
            o = jnp.where(out_lane < hd, res[0][rc], res[1][rc])
            o_ref[rc * rows_per:(rc + 1) * rows_per, :] = o.astype(BF16)

    for c in range(nb):
        pl.when(qi == c)(functools.partial(tile, c))


def _moba_attention(qb, kb, vb, kmean, B, S):
    nb = S // MOBA_BLOCK
    blk = MOBA_BLOCK
    hpairs = MOBA_HEADS // 2
    pair = 2 * HEAD_DIM
    tab = _moba_bias_table(S, nb)
    return pl.pallas_call(
        functools.partial(_moba_kernel, nb=nb),
        grid=(B, hpairs, nb),
        in_specs=[pl.BlockSpec((blk, pair), lambda b, h, q: (b * nb + q, h)),
                  pl.BlockSpec((S, pair), lambda b, h, q: (b, h)),
                  pl.BlockSpec((S, pair), lambda b, h, q: (b, h)),
                  pl.BlockSpec((1, nb, pair), lambda b, h, q: (b, 0, h)),
                  pl.BlockSpec((S, pair), lambda b, h, q: (0, h))],
        out_specs=pl.BlockSpec((blk, pair), lambda b, h, q: (b * nb + q, h)),
        out_shape=jax.ShapeDtypeStruct((B * S, MOBA_W), BF16),
        scratch_shapes=[pltpu.VMEM((S, 2 * pair), BF16),
                        pltpu.VMEM((2, blk // MOBA_ROWS, nb, MOBA_ROWS, blk), F32)],
        compiler_params=pltpu.CompilerParams(dimension_semantics=("parallel", "parallel", "arbitrary"),
                                             vmem_limit_bytes=VMEM_LIMIT_BYTES),
        name="moba_attention",
    )(qb, kb, vb, kmean, tab)


def _route(scores_t, biased, tm):
    gsz = N_EXPERTS // N_GROUPS
    sub = lax.broadcasted_iota(jnp.int32, (gsz, tm), 0)
    grp = [biased[g * gsz:(g + 1) * gsz, :] for g in range(N_GROUPS)]
    gscore = []
    for v in grp:
        m1 = jnp.max(v, axis=0, keepdims=True)
        first = jnp.min(jnp.where(v == m1, sub, gsz), axis=0, keepdims=True)
        m2 = jnp.max(jnp.where(sub == first, REMOVED, v), axis=0, keepdims=True)
        gscore.append(m1 + m2)
    masked = []
    for g in range(N_GROUPS):
        rank = jnp.zeros((1, tm), jnp.int32)
        for g2 in range(N_GROUPS):
            if g2 == g:
                continue
            beats = (gscore[g2] >= gscore[g]) if g2 < g else (gscore[g2] > gscore[g])
            rank = rank + jnp.where(beats, 1, 0)
        masked.append(jnp.where(rank < TOPK_GROUPS, grp[g], NEG_INF))
    v = jnp.concatenate(masked, axis=0)
    e_iota = lax.broadcasted_iota(jnp.int32, (N_EXPERTS, tm), 0)
    picked = jnp.zeros((N_EXPERTS, tm), F32)
    experts, svals = [], []
    for _ in range(TOP_K):
        m = jnp.max(v, axis=0, keepdims=True)
        first = jnp.min(jnp.where(v == m, e_iota, N_EXPERTS), axis=0, keepdims=True)
        pick = e_iota == first
        experts.append(first)
        svals.append(jnp.sum(jnp.where(pick, scores_t, 0.0), axis=0, keepdims=True))
        picked = jnp.where(pick, 1.0, picked)
        v = jnp.where(pick, REMOVED, v)
    denom = svals[0]
    for s in svals[1:]:
        denom = denom + s
    weights = [s / denom * ROUTED_SCALE for s in svals]
    return experts, weights, picked


MOE_TILE = 2048
ROUTE_SHIFT = 12
assert MOE_TILE <= (1 << ROUTE_SHIFT)


def _mix_kernel(x_ref, aa_ref, ab_ref, wg_ref, wa_ref, wb_ref, wo_ref, g_ref, b_ref, wr_ref, rb_ref, tri_ref,
                h_ref, code_ref, wk_ref, cnt_ref, run_sc, *, alpha, tm):
    step = pl.program_id(0) % (MOE_TILE // tm)
    x = x_ref[...]
    xb = x.astype(BF16)
    d = x.shape[-1]
    ga = _dot(xb, wg_ref[:, :d])
    gb = _dot(xb, wg_ref[:, d:])
    y = _sigmoid(ga) * _dot(aa_ref[...], wa_ref[...]) + _sigmoid(gb) * _dot(ab_ref[...], wb_ref[...])
    mix = _dot(y.astype(BF16), wo_ref[...])
    h = _layernorm(alpha * x + mix, g_ref[...], b_ref[...])
    for j in range(d // LANES):
        h_ref[pl.ds(j, tm, stride=d // LANES), :] = h[:, j * LANES:(j + 1) * LANES]
    logits_t = _dot_nt(wr_ref[...], h, precision=lax.Precision.HIGHEST)
    scores_t = _sigmoid(logits_t)
    experts, weights, picked = _route(scores_t, scores_t + rb_ref[...], tm)

    @pl.when(step == 0)
    def _():
        run_sc[...] = jnp.zeros_like(run_sc)

    pos = _dot(picked.astype(BF16), tri_ref[...]) + run_sc[...]
    run_sc[...] += jnp.sum(picked, axis=1, keepdims=True)
    e_iota = lax.broadcasted_iota(jnp.int32, (N_EXPERTS, tm), 0)
    codes = []
    for k in range(TOP_K):
        rank = jnp.sum(jnp.where(e_iota == experts[k], pos, 0.0), axis=0, keepdims=True)
        codes.append(experts[k] * (1 << ROUTE_SHIFT) + rank.astype(jnp.int32))
    for c in range(tm // LANES):
        lanes = slice(c * LANES, (c + 1) * LANES)
        code_ref[0, c] = jnp.concatenate([v[:, lanes] for v in codes], axis=0)
        wk_ref[0, c] = jnp.concatenate([v[:, lanes] for v in weights], axis=0)

    picked_t = jnp.concatenate([picked, jnp.zeros((128 - N_EXPERTS, tm), F32)], axis=0).T
    counts = jnp.sum(picked_t, axis=0, keepdims=True).astype(jnp.int32)

    @pl.when(step == 0)
    def _():
        cnt_ref[0] = counts

    @pl.when(step != 0)
    def _():
        cnt_ref[0] += counts


def _mix_ln_route(xf, att_a, att_b, w_gates, w_a, w_b, w_o, ln_g, ln_b, w_router_t, router_bias, alpha, tm=512):
    T, D = xf.shape
    assert MOE_TILE % tm == 0 and T % MOE_TILE == 0
    steps = MOE_TILE // tm
    tri = jnp.asarray(np.triu(np.ones((tm, tm), np.float32), k=1), BF16)
    row = lambda w: pl.BlockSpec((tm, w), lambda i: (i, 0))
    col = pl.BlockSpec((1, tm // LANES, TOP_K, LANES), lambda i: (i // steps, i % steps, 0, 0))
    rec_shape = (T // MOE_TILE, MOE_TILE // LANES, TOP_K, LANES)
    full = lambda a: pl.BlockSpec(a.shape, lambda i: (0,) * a.ndim)
    return pl.pallas_call(
        functools.partial(_mix_kernel, alpha=alpha, tm=tm),
        grid=(T // tm,),
        in_specs=[row(D), row(SWA_Q_W), row(MOBA_W), full(w_gates), full(w_a), full(w_b), full(w_o),
                  full(ln_g), full(ln_b), full(w_router_t), full(router_bias), full(tri)],
        out_specs=[pl.BlockSpec((tm * (D // LANES), LANES), lambda i: (i, 0)), col, col, pl.BlockSpec((1, 1, 128), lambda i: (i // steps, 0, 0))],
        out_shape=[jax.ShapeDtypeStruct((T * (D // LANES), LANES), F32), jax.ShapeDtypeStruct(rec_shape, jnp.int32),
                   jax.ShapeDtypeStruct(rec_shape, F32), jax.ShapeDtypeStruct((T // MOE_TILE, 1, 128), jnp.int32)],
        scratch_shapes=[pltpu.VMEM((N_EXPERTS, 1), F32)],
        compiler_params=pltpu.CompilerParams(dimension_semantics=("arbitrary",), vmem_limit_bytes=VMEM_LIMIT_BYTES),
        name="mix_ln_route",
    )(xf, att_a, att_b, w_gates, w_a, w_b, w_o, ln_g, ln_b, w_router_t, router_bias, tri)


XY_STRIDE = MOE_TILE + 8
MOE_ROWS = 64
MOE_STATIC_BLOCKS = 12
EDGE_ROWS = 256
GATHER_UNROLL = 16
SCATTER_UNROLL = 8


def _moe_kernel(h_ref, code_ref, wk_ref, cnt_ref, eg_ref, eu_ref, ed_ref, sg_ref, su_ref, sd_ref, g_ref, b_ref,
                o_ref, acc_sc, xy_sc, dst_sc, dst_s, wk_s, cnt_s, off_s, pick_s, sem, *, alpha, ch):
    tile = pl.program_id(0)
    e = pl.program_id(1)
    tt = MOE_TILE

    def token_major_rows(ref, r0, rows):
        start = pl.multiple_of(r0 * ch, ch)
        return jnp.concatenate([ref[pl.ds(start + j, rows, stride=ch), :] for j in range(ch)], axis=1)

    def swiglu(xb, wg, wu, wd):
        a = _dot(xb, wg)
        return _dot((a * _sigmoid(a) * _dot(xb, wu)).astype(BF16), wd)

    @pl.when((tile == 0) & (e == 0))
    def _():
        xy_sc[...] = jnp.zeros_like(xy_sc)

    n_picks = tt * TOP_K
    rec_rows = n_picks // LANES
    rows_per_step = rec_rows // N_EXPERTS
    slot = tile % 2

    def prepare(j, slot_j):
        copies = [pltpu.make_async_copy(cnt_ref.at[j], cnt_s, sem.at[2]),
                  pltpu.make_async_copy(code_ref.at[j], dst_sc, sem.at[0])]
        for c in copies:
            c.start()
        for c in copies:
            c.wait()
        total = jnp.int32(0)
        for x in range(N_EXPERTS):
            off_s[slot_j * LANES + x] = total
            total = total + cnt_s[0, x]
        off_s[slot_j * LANES + N_EXPERTS] = total
        code = dst_sc[...]
        expert = code >> ROUTE_SHIFT
        dst = code & ((1 << ROUTE_SHIFT) - 1)
        for x in range(1, N_EXPERTS):
            dst = dst + jnp.where(expert == x, off_s[slot_j * LANES + x], 0)
        dst_sc[...] = dst
        dst_copy = pltpu.make_async_copy(dst_sc, dst_s, sem.at[0])
        dst_copy.start()
        dst_copy.wait()

    def build_rows(r0, rows, slot_j):
        for rr in range(rows):
            r = r0 + rr
            for u in range(LANES):
                pick_s[slot_j * n_picks + dst_s[r, u]] = r * LANES + u

    @pl.when(e == 0)
    def _():
        wk_copy = pltpu.make_async_copy(wk_ref.at[tile], wk_s, sem.at[1])
        wk_copy.start()

        @pl.when(tile == 0)
        def _():
            prepare(0, 0)
            lax.fori_loop(0, rec_rows, lambda r, c: (build_rows(r, 1, 0), c)[1], 0)

        prepare(jnp.minimum(tile + 1, pl.num_programs(0) - 1), 1 - slot)
        wk_copy.wait()

        def shared_block(rb, carry):
            r0 = rb * EDGE_ROWS
            y = swiglu(token_major_rows(h_ref, r0, EDGE_ROWS).astype(BF16), sg_ref[...], su_ref[...], sd_ref[...])
            start = pl.multiple_of(r0 * ch, ch)
            for j in range(ch):
                acc_sc[pl.ds(start + j, EDGE_ROWS, stride=ch), :] = y[:, j * LANES:(j + 1) * LANES]
            return carry

        lax.fori_loop(0, tt // EDGE_ROWS, shared_block, 0)

    base = off_s[slot * LANES + e]
    n = off_s[slot * LANES + e + 1] - base
    pbase = slot * n_picks + base

    def token_rows(p):
        return pl.multiple_of((p >> 3) << 3, ch)

    def gather(m0, count):
        for u in range(count):
            m = m0 + u
            xy_sc[pl.ds(m, ch, stride=XY_STRIDE), :] = h_ref[pl.ds(token_rows(pick_s[pbase + m]), ch), :]

    n_main = (n // GATHER_UNROLL) * GATHER_UNROLL
    lax.fori_loop(0, n // GATHER_UNROLL, lambda g, c: (gather(g * GATHER_UNROLL, GATHER_UNROLL), c)[1], 0)
    lax.fori_loop(n_main, n, lambda m, c: (gather(m, 1), c)[1], 0)

    def expert_rows(r0, rows):
        xb = jnp.concatenate([xy_sc[pl.ds(j * XY_STRIDE + r0, rows), :] for j in range(ch)], axis=1).astype(BF16)
        y = swiglu(xb, eg_ref[0], eu_ref[0], ed_ref[0])
        for j in range(ch):
            xy_sc[pl.ds(j * XY_STRIDE + r0, rows), :] = y[:, j * LANES:(j + 1) * LANES]

    n_blocks = (n + MOE_ROWS - 1) // MOE_ROWS

    def expert_static(v):
        build_rows(e * rows_per_step, rows_per_step, 1 - slot)
        if v > 0:
            expert_rows(0, v * MOE_ROWS)

    for v in range(MOE_STATIC_BLOCKS + 1):
        pl.when(n_blocks == v)(functools.partial(expert_static, v))

    @pl.when(n_blocks > MOE_STATIC_BLOCKS)
    def _():
        build_rows(e * rows_per_step, rows_per_step, 1 - slot)

        def expert_block(rb, carry):
            expert_rows(pl.multiple_of(rb * MOE_ROWS, MOE_ROWS), MOE_ROWS)
            return carry

        lax.fori_loop(0, n_blocks, expert_block, 0)

    def scatter(m0, count):
        new = []
        for u in range(count):
            m = m0 + u
            p = pick_s[pbase + m]
            t8 = token_rows(p)
            y = xy_sc[pl.ds(m, ch, stride=XY_STRIDE), :]
            new.append((t8, acc_sc[pl.ds(t8, ch), :] + wk_s[p] * y))
        for t8, v in new:
            acc_sc[pl.ds(t8, ch), :] = v

    n_main = (n // SCATTER_UNROLL) * SCATTER_UNROLL
    lax.fori_loop(0, n // SCATTER_UNROLL, lambda g, c: (scatter(g * SCATTER_UNROLL, SCATTER_UNROLL), c)[1], 0)
    lax.fori_loop(n_main, n, lambda m, c: (scatter(m, 1), c)[1], 0)

    @pl.when(e == pl.num_programs(1) - 1)
    def _():
        def final_block(rb, carry):
            r0 = rb * EDGE_ROWS
            z = alpha * token_major_rows(h_ref, r0, EDGE_ROWS) + token_major_rows(acc_sc, r0, EDGE_ROWS)
            o_ref[pl.ds(pl.multiple_of(r0, EDGE_ROWS), EDGE_ROWS), :] = _layernorm(z, g_ref[...], b_ref[...])
            return carry

        lax.fori_loop(0, tt // EDGE_ROWS, final_block, 0)


def _moe(h_tm, code, wk, cnt, eg, eu, ed, sg, su, sd, ln_g, ln_b, alpha):
    E, D, F = eg.shape
    ch = D // LANES
    T = h_tm.shape[0] // ch
    assert ch == 8 and TOP_K == 8, "a token fills one 8-sublane group; pick ids are token * 8 + round"
    tt = MOE_TILE
    nt = T // tt
    rec_rows = tt * TOP_K // LANES
    code = code.transpose(0, 1, 3, 2).reshape(nt, rec_rows, LANES)
    wk = wk.transpose(0, 1, 3, 2).reshape(nt, tt * TOP_K)
    full = lambda a: pl.BlockSpec(a.shape, lambda i, e: (0,) * a.ndim)
    return pl.pallas_call(
        functools.partial(_moe_kernel, alpha=alpha, ch=ch),
        grid=(T // tt, E),
        in_specs=[pl.BlockSpec((tt * ch, LANES), lambda i, e: (i, 0), pipeline_mode=pl.Buffered(1)),
                  pl.BlockSpec(memory_space=pl.ANY),
                  pl.BlockSpec(memory_space=pl.ANY),
                  pl.BlockSpec(memory_space=pl.ANY),
                  pl.BlockSpec((1, D, F), lambda i, e: (e, 0, 0)),
                  pl.BlockSpec((1, D, F), lambda i, e: (e, 0, 0)),
                  pl.BlockSpec((1, F, D), lambda i, e: (e, 0, 0)),
                  full(sg), full(su), full(sd), full(ln_g), full(ln_b)],
        out_specs=pl.BlockSpec((tt, D), lambda i, e: (i, 0)),
        out_shape=jax.ShapeDtypeStruct((T, D), F32),
        scratch_shapes=[pltpu.VMEM((tt * ch, LANES), F32),
                        pltpu.VMEM((ch * XY_STRIDE, LANES), F32),
                        pltpu.VMEM((rec_rows, LANES), jnp.int32),
                        pltpu.SMEM((rec_rows, LANES), jnp.int32),
                        pltpu.SMEM((TOP_K * tt,), F32),
                        pltpu.SMEM((1, LANES), jnp.int32),
                        pltpu.SMEM((2 * LANES,), jnp.int32),
                        pltpu.SMEM((2 * TOP_K * tt,), jnp.int32),
                        pltpu.SemaphoreType.DMA((3,))],
        compiler_params=pltpu.CompilerParams(dimension_semantics=("arbitrary", "arbitrary"),
                                             vmem_limit_bytes=VMEM_LIMIT_BYTES),
        name="moe_experts",
    )(h_tm, code, wk, cnt, eg, eu, ed, sg, su, sd, ln_g, ln_b)


def kernel(x, w_in, swa_sinks, w_branch_swa, w_branch_moba, w_out, ln1_g, ln1_b, w_router, router_bias,
           w_exp_gate, w_exp_up, w_exp_down, w_sh_gate, w_sh_up, w_sh_down, ln2_g, ln2_b):
    B, S, D = x.shape
    depth = w_in.shape[0]
    alpha = (2.0 * depth) ** 0.25
    h = x.reshape(B * S, D)
    for l in range(depth):
        w_in_b = w_in[l].astype(BF16)
        qa, ka, va, qb, kb, vb, kmean = _qkv_proj(h, w_in_b[:, :QKV_COLS])
        att_a = _swa_attention(qa, ka, va, swa_sinks[l], B, S)
        att_b = _moba_attention(qb, kb, vb, kmean.reshape(B, S // MOBA_BLOCK, MOBA_W), B, S)
        h1, code, wk, cnt = _mix_ln_route(
            h, att_a, att_b, w_in_b[:, QKV_COLS:], w_branch_swa[l].astype(BF16), w_branch_moba[l].astype(BF16),
            w_out[l].astype(BF16), ln1_g[l].reshape(1, D), ln1_b[l].reshape(1, D),
            w_router[l].T, router_bias[l].reshape(N_EXPERTS, 1), alpha)
        h = _moe(h1, code, wk, cnt, w_exp_gate[l].astype(BF16), w_exp_up[l].astype(BF16),
                 w_exp_down[l].astype(BF16), w_sh_gate[l].astype(BF16), w_sh_up[l].astype(BF16),
                 w_sh_down[l].astype(BF16), ln2_g[l].reshape(1, D), ln2_b[l].reshape(1, D), alpha)
    return h.reshape(B, S, D)
```

```python
import functools

import numpy as np
import jax
import jax.numpy as jnp
from jax import lax
from jax.experimental import pallas as pl
from jax.experimental.pallas import tpu as pltpu

F32 = jnp.float32
BF16 = jnp.bfloat16

HEAD_DIM = 64
SWA_HEADS = 8
SWA_KV_HEADS = 2
SWA_BLOCK = 128
MOBA_HEADS = 8
MOBA_BLOCK = 256
MOBA_TOPK = 3
N_EXPERTS = 64
TOP_K = 8
N_GROUPS = 8
TOPK_GROUPS = 4
ROUTED_SCALE = 2.5
LN_EPS = 1e-5
NEG_INF = -1e30
REMOVED = -3e38

SWA_Q_W = SWA_HEADS * HEAD_DIM
SWA_KV_W = SWA_KV_HEADS * HEAD_DIM
MOBA_W = MOBA_HEADS * HEAD_DIM
QKV_COLS = SWA_Q_W + 2 * SWA_KV_W + 3 * MOBA_W

_N_SOFTMAX_HEADS = SWA_HEADS + MOBA_HEADS
_SLOPES = np.asarray(2.0 ** (-8.0 * np.arange(1, _N_SOFTMAX_HEADS + 1) / _N_SOFTMAX_HEADS), np.float32)
SWA_SLOPES = [float(s) for s in _SLOPES[:SWA_HEADS]]
MOBA_SLOPES = [float(s) for s in _SLOPES[SWA_HEADS:]]

LANES = 128
VMEM_LIMIT_BYTES = 56 * 1024 * 1024


def _dot(a, b):
    return jnp.dot(a, b, preferred_element_type=F32)


def _dot_nt(a, b, precision=None):
    return lax.dot_general(a, b, (((1,), (1,)), ((), ())), preferred_element_type=F32, precision=precision)


def _sigmoid(x):
    return 1.0 / (1.0 + jnp.exp(-x))


def _layernorm(z, g, b):
    mu = jnp.mean(z, axis=-1, keepdims=True)
    zc = z - mu
    var = jnp.mean(zc * zc, axis=-1, keepdims=True)
    return zc * lax.rsqrt(var + LN_EPS) * g + b


def _qkv_kernel(x_ref, w_ref, qa_ref, ka_ref, va_ref, qb_ref, kb_ref, vb_ref, km_ref, *, tm):
    xb = x_ref[...].astype(BF16)
    scale = HEAD_DIM ** -0.5

    def proj(lo, hi):
        return _dot(xb, w_ref[:, lo:hi])

    c = 0
    qa_ref[...] = (proj(c, c + SWA_Q_W) * scale).astype(BF16)
    c += SWA_Q_W
    ka_ref[...] = proj(c, c + SWA_KV_W).astype(BF16)
    c += SWA_KV_W
    va_ref[...] = proj(c, c + SWA_KV_W).astype(BF16)
    c += SWA_KV_W
    qb_ref[...] = (proj(c, c + MOBA_W) * scale).astype(BF16)
    c += MOBA_W
    kb = proj(c, c + MOBA_W)
    kb_ref[...] = kb.astype(BF16)
    for i in range(tm // MOBA_BLOCK):
        blk = kb[i * MOBA_BLOCK:(i + 1) * MOBA_BLOCK, :]
        km_ref[0, i:i + 1, :] = jnp.sum(blk, axis=0, keepdims=True) * (1.0 / MOBA_BLOCK)
    c += MOBA_W
    vb_ref[...] = proj(c, c + MOBA_W).astype(BF16)


def _qkv_proj(xf, w_qkv, tm=512):
    T, D = xf.shape
    nt = T // tm
    row = lambda w: pl.BlockSpec((tm, w), lambda i: (i, 0))
    outs = pl.pallas_call(
        functools.partial(_qkv_kernel, tm=tm),
        grid=(nt,),
        in_specs=[row(D), pl.BlockSpec((D, QKV_COLS), lambda i: (0, 0))],
        out_specs=[row(SWA_Q_W), row(SWA_KV_W), row(SWA_KV_W), row(MOBA_W), row(MOBA_W), row(MOBA_W),
                   pl.BlockSpec((1, tm // MOBA_BLOCK, MOBA_W), lambda i: (i, 0, 0))],
        out_shape=[jax.ShapeDtypeStruct((T, SWA_Q_W), BF16), jax.ShapeDtypeStruct((T, SWA_KV_W), BF16),
                   jax.ShapeDtypeStruct((T, SWA_KV_W), BF16), jax.ShapeDtypeStruct((T, MOBA_W), BF16),
                   jax.ShapeDtypeStruct((T, MOBA_W), BF16), jax.ShapeDtypeStruct((T, MOBA_W), BF16),
                   jax.ShapeDtypeStruct((nt, tm // MOBA_BLOCK, MOBA_W), F32)],
        compiler_params=pltpu.CompilerParams(dimension_semantics=("parallel",), vmem_limit_bytes=VMEM_LIMIT_BYTES),
        name="qkv_proj",
    )(xf, w_qkv)
    return outs


def _swa_bias_table():
    blk = SWA_BLOCK
    i = np.arange(blk)[:, None]
    j = np.arange(2 * blk)[None, :]
    dist = i + blk - j
    band = (dist >= 0) & (dist < blk)
    alibi = -(np.asarray(SWA_SLOPES, np.float32)[:, None, None] * dist.astype(np.float32)[None])
    tabs = [np.where((band & (j >= blk))[None], alibi, np.float32(NEG_INF)),
            np.where(band[None], alibi, np.float32(NEG_INF))]
    return jnp.asarray(np.stack(tabs).astype(np.float32))


def _swa_kernel(sink_ref, q_ref, kp_ref, ko_ref, vp_ref, vo_ref, bias_ref, o_ref):
    blk = SWA_BLOCK
    k = jnp.concatenate([kp_ref[...], ko_ref[...]], axis=0)
    v = jnp.concatenate([vp_ref[...], vo_ref[...]], axis=0)
    group = SWA_HEADS // SWA_KV_HEADS
    outs = []
    for h in range(SWA_HEADS):
        kv = h // group
        s = _dot_nt(q_ref[:, h * HEAD_DIM:(h + 1) * HEAD_DIM], k[:, kv * HEAD_DIM:(kv + 1) * HEAD_DIM]) + bias_ref[0, h]
        sink = sink_ref[h]
        m = jnp.maximum(jnp.max(jnp.maximum(s[:, :blk], s[:, blk:]), axis=-1, keepdims=True), sink)
        e = jnp.exp(s - m)
        denom = jnp.sum(e[:, :blk] + e[:, blk:], axis=-1, keepdims=True) + jnp.exp(sink - m)
        outs.append(_dot(e.astype(BF16), v[:, kv * HEAD_DIM:(kv + 1) * HEAD_DIM]) / denom)
    o_ref[...] = jnp.concatenate(outs, axis=-1).astype(BF16)


def _swa_attention(qa, ka, va, sinks, B, S):
    nq = S // SWA_BLOCK
    own = lambda b, n: (b * nq + n, 0)
    prev = lambda b, n: (b * nq + jnp.maximum(n - 1, 0), 0)
    kv_spec = lambda im: pl.BlockSpec((SWA_BLOCK, SWA_KV_W), im)
    return pl.pallas_call(
        _swa_kernel,
        grid=(B, nq),
        in_specs=[pl.BlockSpec(memory_space=pltpu.SMEM),
                  pl.BlockSpec((SWA_BLOCK, SWA_Q_W), own),
                  kv_spec(prev), kv_spec(own), kv_spec(prev), kv_spec(own),
                  pl.BlockSpec((1, SWA_HEADS, SWA_BLOCK, 2 * SWA_BLOCK), lambda b, n: (jnp.minimum(n, 1), 0, 0, 0))],
        out_specs=pl.BlockSpec((SWA_BLOCK, SWA_Q_W), own),
        out_shape=jax.ShapeDtypeStruct((B * S, SWA_Q_W), BF16),
        compiler_params=pltpu.CompilerParams(dimension_semantics=("parallel", "parallel")),
        name="swa_attention",
    )(sinks, qa, ka, ka, va, va, _swa_bias_table())


MOBA_ROWS = 256


def _moba_bias_table(S, nb):
    pos = np.arange(S)
    onehot = np.zeros((S, MOBA_HEADS, HEAD_DIM), np.float32)
    onehot[pos, :, pos // MOBA_BLOCK] = 1.0
    tab = jnp.asarray(onehot)
    rem = jnp.asarray(np.asarray(MOBA_SLOPES, np.float32)[None, :] * pos[:, None].astype(np.float32))
    for t in range(3):
        part = rem.astype(BF16).astype(F32)
        tab = tab.at[:, :, nb + t].set(part)
        rem = rem - part
    return tab.reshape(S, MOBA_HEADS * HEAD_DIM).astype(BF16)


def _moba_kernel(q_ref, k_ref, v_ref, km_ref, tab_ref, o_ref, kaug_sc, s_sc, *, nb):
    qi = pl.program_id(2)
    blk = MOBA_BLOCK
    hd = HEAD_DIM
    rows_per = MOBA_ROWS

    @pl.when(qi == 0)
    def _():
        for hh in range(2):
            kaug_sc[:, 2 * hh * hd:(2 * hh + 1) * hd] = k_ref[:, hh * hd:(hh + 1) * hd]
            kaug_sc[:, (2 * hh + 1) * hd:(2 * hh + 2) * hd] = tab_ref[:, hh * hd:(hh + 1) * hd]

    ri = lax.broadcasted_iota(jnp.int32, (rows_per, blk), 0)
    cj = lax.broadcasted_iota(jnp.int32, (rows_per, blk), 1)
    n_iota = lax.broadcasted_iota(jnp.int32, (nb, blk), 0)
    lane64 = lax.broadcasted_iota(jnp.int32, (1, hd), 1)
    ones_row = jnp.where((lane64 >= nb) & (lane64 < nb + 3), 1.0, 0.0)
    out_lane = lax.broadcasted_iota(jnp.int32, (rows_per, 2 * hd), 1)

    def tile(c):
        q_augs = []
        for hh in range(2):
            lanes = slice(hh * hd, (hh + 1) * hd)
            qh = q_ref[:, lanes]
            g_t = _dot_nt(km_ref[0, :, lanes], qh.astype(F32), precision=lax.Precision.HIGHEST)
            rank = jnp.zeros((nb, blk), jnp.int32)
            for mblk in range(c):
                gm = g_t[mblk:mblk + 1, :]
                beats = (gm > g_t) | ((gm == g_t) & (mblk < n_iota))
                rank = rank + jnp.where(beats, 1, 0)
            keep = ((n_iota < c) & (rank < MOBA_TOPK)) | (n_iota == c)
            selb = jnp.where(keep, 0.0, NEG_INF)
            selb_t = jnp.concatenate([selb, jnp.zeros((128 - nb, blk), F32)], axis=0).T
            q_bias = (selb_t[:, :hd] + ones_row).astype(BF16)
            q_augs.append(jnp.concatenate([qh, q_bias], axis=1))
        res = []
        for hh in range(2):
            q_aug = q_augs[hh]
            per_chunk = []
            for rc in range(blk // rows_per):
                qa = q_aug[rc * rows_per:(rc + 1) * rows_per]
                mx = None
                for n in range(c + 1):
                    s = _dot_nt(qa, kaug_sc[n * blk:(n + 1) * blk, 2 * hh * hd:(2 * hh + 2) * hd])
                    if n == c:
                        s = jnp.where(ri + rc * rows_per >= cj, s, NEG_INF)
                    s_sc[hh, rc, n] = s
                    t = jnp.maximum(s[:, :128], s[:, 128:])
                    mx = t if mx is None else jnp.maximum(mx, t)
                m = jnp.max(mx, axis=-1, keepdims=True)
                lsum = None
                acc = None
                for n in range(c + 1):
                    p = jnp.exp(s_sc[hh, rc, n] - m)
                    t = p[:, :128] + p[:, 128:]
                    lsum = t if lsum is None else lsum + t
                    pv = _dot(p.astype(BF16), v_ref[n * blk:(n + 1) * blk, :])
                    acc = pv if acc is None else acc + pv
                per_chunk.append(acc / jnp.sum(lsum, axis=-1, keepdims=True))
            res.append(per_chunk)
        for rc in range(blk // rows_per):
            o = jnp.where(out_lane < hd, res[0][rc], res[1][rc])
            o_ref[rc * rows_per:(rc + 1) * rows_per, :] = o.astype(BF16)

    for c in range(nb):
        pl.when(qi == c)(functools.partial(tile, c))


def _moba_attention(qb, kb, vb, kmean, B, S):
    nb = S // MOBA_BLOCK
    blk = MOBA_BLOCK
    hpairs = MOBA_HEADS // 2
    pair = 2 * HEAD_DIM
    tab = _moba_bias_table(S, nb)
    return pl.pallas_call(
        functools.partial(_moba_kernel, nb=nb),
        grid=(B, hpairs, nb),
        in_specs=[pl.BlockSpec((blk, pair), lambda b, h, q: (b * nb + q, h)),
                  pl.BlockSpec((S, pair), lambda b, h, q: (b, h)),
                  pl.BlockSpec((S, pair), lambda b, h, q: (b, h)),
                  pl.BlockSpec((1, nb, pair), lambda b, h, q: (b, 0, h)),
                  pl.BlockSpec((S, pair), lambda b, h, q: (0, h))],
        out_specs=pl.BlockSpec((blk, pair), lambda b, h, q: (b * nb + q, h)),
        out_shape=jax.ShapeDtypeStruct((B * S, MOBA_W), BF16),
        scratch_shapes=[pltpu.VMEM((S, 2 * pair), BF16),
                        pltpu.VMEM((2, blk // MOBA_ROWS, nb, MOBA_ROWS, blk), F32)],
        compiler_params=pltpu.CompilerParams(dimension_semantics=("parallel", "parallel", "arbitrary"),
                                             vmem_limit_bytes=VMEM_LIMIT_BYTES),
        name="moba_attention",
    )(qb, kb, vb, kmean, tab)


def _route(scores_t, biased, tm):
    gsz = N_EXPERTS // N_GROUPS
    sub = lax.broadcasted_iota(jnp.int32, (gsz, tm), 0)
    grp = [biased[g * gsz:(g + 1) * gsz, :] for g in range(N_GROUPS)]
    gscore = []
    for v in grp:
        m1 = jnp.max(v, axis=0, keepdims=True)
        first = jnp.min(jnp.where(v == m1, sub, gsz), axis=0, keepdims=True)
        m2 = jnp.max(jnp.where(sub == first, REMOVED, v), axis=0, keepdims=True)
        gscore.append(m1 + m2)
    masked = []
    for g in range(N_GROUPS):
        rank = jnp.zeros((1, tm), jnp.int32)
        for g2 in range(N_GROUPS):
            if g2 == g:
                continue
            beats = (gscore[g2] >= gscore[g]) if g2 < g else (gscore[g2] > gscore[g])
            rank = rank + jnp.where(beats, 1, 0)
        masked.append(jnp.where(rank < TOPK_GROUPS, grp[g], NEG_INF))
    v = jnp.concatenate(masked, axis=0)
    e_iota = lax.broadcasted_iota(jnp.int32, (N_EXPERTS, tm), 0)
    picked = jnp.zeros((N_EXPERTS, tm), F32)
    experts, svals = [], []
    for _ in range(TOP_K):
        m = jnp.max(v, axis=0, keepdims=True)
        first = jnp.min(jnp.where(v == m, e_iota, N_EXPERTS), axis=0, keepdims=True)
        pick = e_iota == first
        experts.append(first)
        svals.append(jnp.sum(jnp.where(pick, scores_t, 0.0), axis=0, keepdims=True))
        picked = jnp.where(pick, 1.0, picked)
        v = jnp.where(pick, REMOVED, v)
    denom = svals[0]
    for s in svals[1:]:
        denom = denom + s
    weights = [s / denom * ROUTED_SCALE for s in svals]
    return experts, weights, picked


MOE_TILE = 2048
ROUTE_SHIFT = 12
assert MOE_TILE <= (1 << ROUTE_SHIFT)


def _mix_kernel(x_ref, aa_ref, ab_ref, wg_ref, wa_ref, wb_ref, wo_ref, g_ref, b_ref, wr_ref, rb_ref, tri_ref,
                h_ref, code_ref, wk_ref, cnt_ref, run_sc, *, alpha, tm):
    step = pl.program_id(0) % (MOE_TILE // tm)
    x = x_ref[...]
    xb = x.astype(BF16)
    d = x.shape[-1]
    ga = _dot(xb, wg_ref[:, :d])
    gb = _dot(xb, wg_ref[:, d:])
    y = _sigmoid(ga) * _dot(aa_ref[...], wa_ref[...]) + _sigmoid(gb) * _dot(ab_ref[...], wb_ref[...])
    mix = _dot(y.astype(BF16), wo_ref[...])
    h = _layernorm(alpha * x + mix, g_ref[...], b_ref[...])
    for j in range(d // LANES):
        h_ref[pl.ds(j, tm, stride=d // LANES), :] = h[:, j * LANES:(j + 1) * LANES]
    logits_t = _dot_nt(wr_ref[...], h, precision=lax.Precision.HIGHEST)
    scores_t = _sigmoid(logits_t)
    experts, weights, picked = _route(scores_t, scores_t + rb_ref[...], tm)

    @pl.when(step == 0)
    def _():
        run_sc[...] = jnp.zeros_like(run_sc)

    pos = _dot(picked.astype(BF16), tri_ref[...]) + run_sc[...]
    run_sc[...] += jnp.sum(picked, axis=1, keepdims=True)
    e_iota = lax.broadcasted_iota(jnp.int32, (N_EXPERTS, tm), 0)
    codes = []
    for k in range(TOP_K):
        rank = jnp.sum(jnp.where(e_iota == experts[k], pos, 0.0), axis=0, keepdims=True)
        codes.append(experts[k] * (1 << ROUTE_SHIFT) + rank.astype(jnp.int32))
    for c in range(tm // LANES):
        lanes = slice(c * LANES, (c + 1) * LANES)
        code_ref[0, c] = jnp.concatenate([v[:, lanes] for v in codes], axis=0)
        wk_ref[0, c] = jnp.concatenate([v[:, lanes] for v in weights], axis=0)

    picked_t = jnp.concatenate([picked, jnp.zeros((128 - N_EXPERTS, tm), F32)], axis=0).T
    counts = jnp.sum(picked_t, axis=0, keepdims=True).astype(jnp.int32)

    @pl.when(step == 0)
    def _():
        cnt_ref[0] = counts

    @pl.when(step != 0)
    def _():
        cnt_ref[0] += counts


def _mix_ln_route(xf, att_a, att_b, w_gates, w_a, w_b, w_o, ln_g, ln_b, w_router_t, router_bias, alpha, tm=512):
    T, D = xf.shape
    assert MOE_TILE % tm == 0 and T % MOE_TILE == 0
    steps = MOE_TILE // tm
    tri = jnp.asarray(np.triu(np.ones((tm, tm), np.float32), k=1), BF16)
    row = lambda w: pl.BlockSpec((tm, w), lambda i: (i, 0))
    col = pl.BlockSpec((1, tm // LANES, TOP_K, LANES), lambda i: (i // steps, i % steps, 0, 0))
    rec_shape = (T // MOE_TILE, MOE_TILE // LANES, TOP_K, LANES)
    full = lambda a: pl.BlockSpec(a.shape, lambda i: (0,) * a.ndim)
    return pl.pallas_call(
        functools.partial(_mix_kernel, alpha=alpha, tm=tm),
        grid=(T // tm,),
        in_specs=[row(D), row(SWA_Q_W), row(MOBA_W), full(w_gates), full(w_a), full(w_b), full(w_o),
                  full(ln_g), full(ln_b), full(w_router_t), full(router_bias), full(tri)],
        out_specs=[pl.BlockSpec((tm * (D // LANES), LANES), lambda i: (i, 0)), col, col, pl.BlockSpec((1, 1, 128), lambda i: (i // steps, 0, 0))],
        out_shape=[jax.ShapeDtypeStruct((T * (D // LANES), LANES), F32), jax.ShapeDtypeStruct(rec_shape, jnp.int32),
                   jax.ShapeDtypeStruct(rec_shape, F32), jax.ShapeDtypeStruct((T // MOE_TILE, 1, 128), jnp.int32)],
        scratch_shapes=[pltpu.VMEM((N_EXPERTS, 1), F32)],
        compiler_params=pltpu.CompilerParams(dimension_semantics=("arbitrary",), vmem_limit_bytes=VMEM_LIMIT_BYTES),
        name="mix_ln_route",
    )(xf, att_a, att_b, w_gates, w_a, w_b, w_o, ln_g, ln_b, w_router_t, router_bias, tri)


XY_STRIDE = MOE_TILE + 8
MOE_ROWS = 64
MOE_STATIC_BLOCKS = 12
MOE_PREFETCH_ROWS = 256
EDGE_ROWS = 256
GATHER_UNROLL = 16
SCATTER_UNROLL = 8


def _moe_kernel(h_ref, code_ref, wk_ref, cnt_ref, eg_ref, eu_ref, ed_ref, sg_ref, su_ref, sd_ref, g_ref, b_ref,
                o_ref, acc_sc, xy_a, xy_b, dst_sc, dst_s, wk_s, cnt_s, off_s, pick_s, sem, *, alpha, ch):
    tile = pl.program_id(0)
    pair = pl.program_id(1)
    tt = MOE_TILE

    def token_major_rows(ref, r0, rows):
        start = pl.multiple_of(r0 * ch, ch)
        return jnp.concatenate([ref[pl.ds(start + j, rows, stride=ch), :] for j in range(ch)], axis=1)

    def swiglu(xb, wg, wu, wd):
        a = _dot(xb, wg)
        return _dot((a * _sigmoid(a) * _dot(xb, wu)).astype(BF16), wd)

    n_picks = tt * TOP_K
    rec_rows = n_picks // LANES
    rows_per_expert = rec_rows // N_EXPERTS
    slot = tile % 2

    @pl.when((tile == 0) & (pair == 0))
    def _():
        xy_a[...] = jnp.zeros_like(xy_a)
        xy_b[...] = jnp.zeros_like(xy_b)
        for u in range(MOE_PREFETCH_ROWS):
            pick_s[2 * n_picks + u] = 0

    def prepare(j, slot_j):
        copies = [pltpu.make_async_copy(cnt_ref.at[j], cnt_s, sem.at[2]),
                  pltpu.make_async_copy(code_ref.at[j], dst_sc, sem.at[0])]
        for c in copies:
            c.start()
        for c in copies:
            c.wait()
        total = jnp.int32(0)
        for x in range(N_EXPERTS):
            off_s[slot_j * LANES + x] = total
            total = total + cnt_s[0, x]
        off_s[slot_j * LANES + N_EXPERTS] = total
        code = dst_sc[...]
        expert = code >> ROUTE_SHIFT
        dst = code & ((1 << ROUTE_SHIFT) - 1)
        for x in range(1, N_EXPERTS):
            dst = dst + jnp.where(expert == x, off_s[slot_j * LANES + x], 0)
        dst_sc[...] = dst
        dst_copy = pltpu.make_async_copy(dst_sc, dst_s, sem.at[0])
        dst_copy.start()
        dst_copy.wait()

    def build_rows(r0, rows, slot_j):
        for rr in range(rows):
            r = r0 + rr
            for u in range(LANES):
                pick_s[slot_j * n_picks + dst_s[r, u]] = r * LANES + u

    @pl.when(pair == 0)
    def _():
        wk_copy = pltpu.make_async_copy(wk_ref.at[tile], wk_s, sem.at[1])
        wk_copy.start()

        @pl.when(tile == 0)
        def _():
            prepare(0, 0)
            lax.fori_loop(0, rec_rows, lambda r, c: (build_rows(r, 1, 0), c)[1], 0)

        prepare(jnp.minimum(tile + 1, pl.num_programs(0) - 1), 1 - slot)
        wk_copy.wait()

        def shared_block(rb, carry):
            r0 = rb * EDGE_ROWS
            y = swiglu(token_major_rows(h_ref, r0, EDGE_ROWS).astype(BF16), sg_ref[...], su_ref[...], sd_ref[...])
            start = pl.multiple_of(r0 * ch, ch)
            for j in range(ch):
                acc_sc[pl.ds(start + j, EDGE_ROWS, stride=ch), :] = y[:, j * LANES:(j + 1) * LANES]
            return carry

        lax.fori_loop(0, tt // EDGE_ROWS, shared_block, 0)

    def expert_list(x):
        first = off_s[slot * LANES + x]
        return slot * n_picks + first, off_s[slot * LANES + x + 1] - first

    def token_rows(p):
        return pl.multiple_of((p >> 3) << 3, ch)

    def gather(buf, first, m0, count):
        for u in range(count):
            m = m0 + u
            buf[pl.ds(m, ch, stride=XY_STRIDE), :] = h_ref[pl.ds(token_rows(pick_s[first + m]), ch), :]

    def gather_range(buf, first, start, stop):
        groups = (stop - start) // GATHER_UNROLL
        lax.fori_loop(0, groups, lambda g, c: (gather(buf, first, start + g * GATHER_UNROLL, GATHER_UNROLL), c)[1], 0)
        lax.fori_loop(start + groups * GATHER_UNROLL, stop, lambda m, c: (gather(buf, first, m, 1), c)[1], 0)

    def run_expert(e, w, cur, nxt):
        pbase, n = expert_list(e)
        pbase_next, n_next = expert_list(jnp.minimum(e + 1, N_EXPERTS - 1))

        def expert_rows(r0, rows):
            xb = jnp.concatenate([cur[pl.ds(j * XY_STRIDE + r0, rows), :] for j in range(ch)], axis=1).astype(BF16)
            y = swiglu(xb, eg_ref[w], eu_ref[w], ed_ref[w])
            for j in range(ch):
                cur[pl.ds(j * XY_STRIDE + r0, rows), :] = y[:, j * LANES:(j + 1) * LANES]

        n_blocks = (n + MOE_ROWS - 1) // MOE_ROWS

        def expert_static(v):
            build_rows(e * rows_per_expert, rows_per_expert, 1 - slot)
            gather(nxt, pbase_next, 0, min(v * MOE_ROWS, MOE_PREFETCH_ROWS))
            if v > 0:
                expert_rows(0, v * MOE_ROWS)

        for v in range(MOE_STATIC_BLOCKS + 1):
            pl.when(n_blocks == v)(functools.partial(expert_static, v))

        @pl.when(n_blocks > MOE_STATIC_BLOCKS)
        def _():
            build_rows(e * rows_per_expert, rows_per_expert, 1 - slot)

            def expert_block(rb, carry):
                expert_rows(pl.multiple_of(rb * MOE_ROWS, MOE_ROWS), MOE_ROWS)
                return carry

            lax.fori_loop(0, n_blocks, expert_block, 0)

        staged = jnp.where(n_blocks > MOE_STATIC_BLOCKS, 0, jnp.minimum(n_blocks * MOE_ROWS, MOE_PREFETCH_ROWS))
        gather_range(nxt, pbase_next, jnp.minimum(staged, n_next), n_next)

        def scatter(m0, count):
            new = []
            for u in range(count):
                m = m0 + u
                p = pick_s[pbase + m]
                t8 = token_rows(p)
                y = cur[pl.ds(m, ch, stride=XY_STRIDE), :]
                new.append((t8, acc_sc[pl.ds(t8, ch), :] + wk_s[p] * y))
            for t8, v in new:
                acc_sc[pl.ds(t8, ch), :] = v

        n_main = (n // SCATTER_UNROLL) * SCATTER_UNROLL
        lax.fori_loop(0, n // SCATTER_UNROLL, lambda g, c: (scatter(g * SCATTER_UNROLL, SCATTER_UNROLL), c)[1], 0)
        lax.fori_loop(n_main, n, lambda m, c: (scatter(m, 1), c)[1], 0)

    @pl.when(pair == 0)
    def _():
        first, count = expert_list(0)
        gather_range(xy_a, first, 0, count)

    run_expert(2 * pair, 0, xy_a, xy_b)
    run_expert(2 * pair + 1, 1, xy_b, xy_a)

    @pl.when(pair == pl.num_programs(1) - 1)
    def _():
        def final_block(rb, carry):
            r0 = rb * EDGE_ROWS
            z = alpha * token_major_rows(h_ref, r0, EDGE_ROWS) + token_major_rows(acc_sc, r0, EDGE_ROWS)
            o_ref[pl.ds(pl.multiple_of(r0, EDGE_ROWS), EDGE_ROWS), :] = _layernorm(z, g_ref[...], b_ref[...])
            return carry

        lax.fori_loop(0, tt // EDGE_ROWS, final_block, 0)


def _moe(h_tm, code, wk, cnt, eg, eu, ed, sg, su, sd, ln_g, ln_b, alpha):
    E, D, F = eg.shape
    ch = D // LANES
    T = h_tm.shape[0] // ch
    assert ch == 8 and TOP_K == 8, "a token fills one 8-sublane group; pick ids are token * 8 + round"
    tt = MOE_TILE
    nt = T // tt
    rec_rows = tt * TOP_K // LANES
    code = code.transpose(0, 1, 3, 2).reshape(nt, rec_rows, LANES)
    wk = wk.transpose(0, 1, 3, 2).reshape(nt, tt * TOP_K)
    full = lambda a: pl.BlockSpec(a.shape, lambda i, e: (0,) * a.ndim)
    return pl.pallas_call(
        functools.partial(_moe_kernel, alpha=alpha, ch=ch),
        grid=(T // tt, E // 2),
        in_specs=[pl.BlockSpec((tt * ch, LANES), lambda i, e: (i, 0), pipeline_mode=pl.Buffered(1)),
                  pl.BlockSpec(memory_space=pl.ANY),
                  pl.BlockSpec(memory_space=pl.ANY),
                  pl.BlockSpec(memory_space=pl.ANY),
                  pl.BlockSpec((2, D, F), lambda i, e: (e, 0, 0)),
                  pl.BlockSpec((2, D, F), lambda i, e: (e, 0, 0)),
                  pl.BlockSpec((2, F, D), lambda i, e: (e, 0, 0)),
                  full(sg), full(su), full(sd), full(ln_g), full(ln_b)],
        out_specs=pl.BlockSpec((tt, D), lambda i, e: (i, 0), pipeline_mode=pl.Buffered(1)),
        out_shape=jax.ShapeDtypeStruct((T, D), F32),
        scratch_shapes=[pltpu.VMEM((tt * ch, LANES), F32),
                        pltpu.VMEM((ch * XY_STRIDE, LANES), F32),
                        pltpu.VMEM((ch * XY_STRIDE, LANES), F32),
                        pltpu.VMEM((rec_rows, LANES), jnp.int32),
                        pltpu.SMEM((rec_rows, LANES), jnp.int32),
                        pltpu.SMEM((TOP_K * tt,), F32),
                        pltpu.SMEM((1, LANES), jnp.int32),
                        pltpu.SMEM((2 * LANES,), jnp.int32),
                        pltpu.SMEM((2 * TOP_K * tt + MOE_PREFETCH_ROWS,), jnp.int32),
                        pltpu.SemaphoreType.DMA((3,))],
        compiler_params=pltpu.CompilerParams(dimension_semantics=("arbitrary", "arbitrary"),
                                             vmem_limit_bytes=VMEM_LIMIT_BYTES),
        name="moe_experts",
    )(h_tm, code, wk, cnt, eg, eu, ed, sg, su, sd, ln_g, ln_b)


def kernel(x, w_in, swa_sinks, w_branch_swa, w_branch_moba, w_out, ln1_g, ln1_b, w_router, router_bias,
           w_exp_gate, w_exp_up, w_exp_down, w_sh_gate, w_sh_up, w_sh_down, ln2_g, ln2_b):
    B, S, D = x.shape
    depth = w_in.shape[0]
    alpha = (2.0 * depth) ** 0.25
    h = x.reshape(B * S, D)
    for l in range(depth):
        w_in_b = w_in[l].astype(BF16)
        qa, ka, va, qb, kb, vb, kmean = _qkv_proj(h, w_in_b[:, :QKV_COLS])
        att_a = _swa_attention(qa, ka, va, swa_sinks[l], B, S)
        att_b = _moba_attention(qb, kb, vb, kmean.reshape(B, S // MOBA_BLOCK, MOBA_W), B, S)
        h1, code, wk, cnt = _mix_ln_route(
            h, att_a, att_b, w_in_b[:, QKV_COLS:], w_branch_swa[l].astype(BF16), w_branch_moba[l].astype(BF16),
            w_out[l].astype(BF16), ln1_g[l].reshape(1, D), ln1_b[l].reshape(1, D),
            w_router[l].T, router_bias[l].reshape(N_EXPERTS, 1), alpha)
        h = _moe(h1, code, wk, cnt, w_exp_gate[l].astype(BF16), w_exp_up[l].astype(BF16),
                 w_exp_down[l].astype(BF16), w_sh_gate[l].astype(BF16), w_sh_up[l].astype(BF16),
                 w_sh_down[l].astype(BF16), ln2_g[l].reshape(1, D), ln2_b[l].reshape(1, D), alpha)
    return h.reshape(B, S, D)
```

```python
import functools

import numpy as np
import jax
import jax.numpy as jnp
from jax import lax
from jax.experimental import pallas as pl
from jax.experimental.pallas import tpu as pltpu

F32 = jnp.float32
BF16 = jnp.bfloat16

HEAD_DIM = 64
SWA_HEADS = 8
SWA_KV_HEADS = 2
SWA_BLOCK = 128
MOBA_HEADS = 8
MOBA_BLOCK = 256
MOBA_TOPK = 3
N_EXPERTS = 64
TOP_K = 8
N_GROUPS = 8
TOPK_GROUPS = 4
ROUTED_SCALE = 2.5
LN_EPS = 1e-5
NEG_INF = -1e30
REMOVED = -3e38

SWA_Q_W = SWA_HEADS * HEAD_DIM
SWA_KV_W = SWA_KV_HEADS * HEAD_DIM
MOBA_W = MOBA_HEADS * HEAD_DIM
QKV_COLS = SWA_Q_W + 2 * SWA_KV_W + 3 * MOBA_W

_N_SOFTMAX_HEADS = SWA_HEADS + MOBA_HEADS
_SLOPES = np.asarray(2.0 ** (-8.0 * np.arange(1, _N_SOFTMAX_HEADS + 1) / _N_SOFTMAX_HEADS), np.float32)
SWA_SLOPES = [float(s) for s in _SLOPES[:SWA_HEADS]]
MOBA_SLOPES = [float(s) for s in _SLOPES[SWA_HEADS:]]

LANES = 128
VMEM_LIMIT_BYTES = 56 * 1024 * 1024


def _dot(a, b):
    return jnp.dot(a, b, preferred_element_type=F32)


def _dot_nt(a, b, precision=None):
    return lax.dot_general(a, b, (((1,), (1,)), ((), ())), preferred_element_type=F32, precision=precision)


def _sigmoid(x):
    return 1.0 / (1.0 + jnp.exp(-x))


def _layernorm(z, g, b):
    mu = jnp.mean(z, axis=-1, keepdims=True)
    zc = z - mu
    var = jnp.mean(zc * zc, axis=-1, keepdims=True)
    return zc * lax.rsqrt(var + LN_EPS) * g + b


def _qkv_kernel(x_ref, w_ref, qa_ref, ka_ref, va_ref, qb_ref, kb_ref, vb_ref, km_ref, *, tm):
    xb = x_ref[...].astype(BF16)
    scale = HEAD_DIM ** -0.5

    def proj(lo, hi):
        return _dot(xb, w_ref[:, lo:hi])

    c = 0
    qa_ref[...] = (proj(c, c + SWA_Q_W) * scale).astype(BF16)
    c += SWA_Q_W
    ka_ref[...] = proj(c, c + SWA_KV_W).astype(BF16)
    c += SWA_KV_W
    va_ref[...] = proj(c, c + SWA_KV_W).astype(BF16)
    c += SWA_KV_W
    qb_ref[...] = (proj(c, c + MOBA_W) * scale).astype(BF16)
    c += MOBA_W
    kb = proj(c, c + MOBA_W)
    kb_ref[...] = kb.astype(BF16)
    for i in range(tm // MOBA_BLOCK):
        blk = kb[i * MOBA_BLOCK:(i + 1) * MOBA_BLOCK, :]
        km_ref[0, i:i + 1, :] = jnp.sum(blk, axis=0, keepdims=True) * (1.0 / MOBA_BLOCK)
    c += MOBA_W
    vb_ref[...] = proj(c, c + MOBA_W).astype(BF16)


def _qkv_proj(xf, w_qkv, tm=512):
    T, D = xf.shape
    nt = T // tm
    row = lambda w: pl.BlockSpec((tm, w), lambda i: (i, 0))
    outs = pl.pallas_call(
        functools.partial(_qkv_kernel, tm=tm),
        grid=(nt,),
        in_specs=[row(D), pl.BlockSpec((D, QKV_COLS), lambda i: (0, 0))],
        out_specs=[row(SWA_Q_W), row(SWA_KV_W), row(SWA_KV_W), row(MOBA_W), row(MOBA_W), row(MOBA_W),
                   pl.BlockSpec((1, tm // MOBA_BLOCK, MOBA_W), lambda i: (i, 0, 0))],
        out_shape=[jax.ShapeDtypeStruct((T, SWA_Q_W), BF16), jax.ShapeDtypeStruct((T, SWA_KV_W), BF16),
                   jax.ShapeDtypeStruct((T, SWA_KV_W), BF16), jax.ShapeDtypeStruct((T, MOBA_W), BF16),
                   jax.ShapeDtypeStruct((T, MOBA_W), BF16), jax.ShapeDtypeStruct((T, MOBA_W), BF16),
                   jax.ShapeDtypeStruct((nt, tm // MOBA_BLOCK, MOBA_W), F32)],
        compiler_params=pltpu.CompilerParams(dimension_semantics=("parallel",), vmem_limit_bytes=VMEM_LIMIT_BYTES),
        name="qkv_proj",
    )(xf, w_qkv)
    return outs


def _swa_bias_table():
    blk = SWA_BLOCK
    i = np.arange(blk)[:, None]
    j = np.arange(2 * blk)[None, :]
    dist = i + blk - j
    band = (dist >= 0) & (dist < blk)
    alibi = -(np.asarray(SWA_SLOPES, np.float32)[:, None, None] * dist.astype(np.float32)[None])
    tabs = [np.where((band & (j >= blk))[None], alibi, np.float32(NEG_INF)),
            np.where(band[None], alibi, np.float32(NEG_INF))]
    return jnp.asarray(np.stack(tabs).astype(np.float32))


def _swa_kernel(sink_ref, q_ref, kp_ref, ko_ref, vp_ref, vo_ref, bias_ref, o_ref):
    blk = SWA_BLOCK
    k = jnp.concatenate([kp_ref[...], ko_ref[...]], axis=0)
    v = jnp.concatenate([vp_ref[...], vo_ref[...]], axis=0)
    group = SWA_HEADS // SWA_KV_HEADS
    heads = range(SWA_HEADS)
    kvs = [h // group for h in heads]
    s = [_dot_nt(q_ref[:, h * HEAD_DIM:(h + 1) * HEAD_DIM], k[:, kvs[h] * HEAD_DIM:(kvs[h] + 1) * HEAD_DIM])
         + bias_ref[0, h] for h in heads]
    m = [jnp.maximum(jnp.max(jnp.maximum(s[h][:, :blk], s[h][:, blk:]), axis=-1, keepdims=True), sink_ref[h])
         for h in heads]
    e = [jnp.exp(s[h] - m[h]) for h in heads]
    denom = [jnp.sum(e[h][:, :blk] + e[h][:, blk:], axis=-1, keepdims=True) + jnp.exp(sink_ref[h] - m[h])
             for h in heads]
    outs = [_dot(e[h].astype(BF16), v[:, kvs[h] * HEAD_DIM:(kvs[h] + 1) * HEAD_DIM]) / denom[h] for h in heads]
    o_ref[...] = jnp.concatenate(outs, axis=-1).astype(BF16)


def _swa_attention(qa, ka, va, sinks, B, S):
    nq = S // SWA_BLOCK
    own = lambda b, n: (b * nq + n, 0)
    prev = lambda b, n: (b * nq + jnp.maximum(n - 1, 0), 0)
    kv_spec = lambda im: pl.BlockSpec((SWA_BLOCK, SWA_KV_W), im)
    return pl.pallas_call(
        _swa_kernel,
        grid=(B, nq),
        in_specs=[pl.BlockSpec(memory_space=pltpu.SMEM),
                  pl.BlockSpec((SWA_BLOCK, SWA_Q_W), own),
                  kv_spec(prev), kv_spec(own), kv_spec(prev), kv_spec(own),
                  pl.BlockSpec((1, SWA_HEADS, SWA_BLOCK, 2 * SWA_BLOCK), lambda b, n: (jnp.minimum(n, 1), 0, 0, 0))],
        out_specs=pl.BlockSpec((SWA_BLOCK, SWA_Q_W), own),
        out_shape=jax.ShapeDtypeStruct((B * S, SWA_Q_W), BF16),
        compiler_params=pltpu.CompilerParams(dimension_semantics=("parallel", "parallel")),
        name="swa_attention",
    )(sinks, qa, ka, ka, va, va, _swa_bias_table())


MOBA_ROWS = 256


def _moba_bias_table(S, nb):
    pos = np.arange(S)
    onehot = np.zeros((S, MOBA_HEADS, HEAD_DIM), np.float32)
    onehot[pos, :, pos // MOBA_BLOCK] = 1.0
    tab = jnp.asarray(onehot)
    rem = jnp.asarray(np.asarray(MOBA_SLOPES, np.float32)[None, :] * pos[:, None].astype(np.float32))
    for t in range(3):
        part = rem.astype(BF16).astype(F32)
        tab = tab.at[:, :, nb + t].set(part)
        rem = rem - part
    return tab.reshape(S, MOBA_HEADS * HEAD_DIM).astype(BF16)


def _moba_kernel(q_ref, k_ref, v_ref, km_ref, tab_ref, o_ref, kaug_sc, s_sc0, s_sc1, *, nb):
    s_scs = (s_sc0, s_sc1)
    qi = pl.program_id(2)
    blk = MOBA_BLOCK
    hd = HEAD_DIM
    rows_per = MOBA_ROWS

    @pl.when(qi == 0)
    def _():
        for hh in range(2):
            kaug_sc[:, 2 * hh * hd:(2 * hh + 1) * hd] = k_ref[:, hh * hd:(hh + 1) * hd]
            kaug_sc[:, (2 * hh + 1) * hd:(2 * hh + 2) * hd] = tab_ref[:, hh * hd:(hh + 1) * hd]

    ri = lax.broadcasted_iota(jnp.int32, (rows_per, blk), 0)
    cj = lax.broadcasted_iota(jnp.int32, (rows_per, blk), 1)
    n_iota = lax.broadcasted_iota(jnp.int32, (nb, blk), 0)
    lane64 = lax.broadcasted_iota(jnp.int32, (1, hd), 1)
    ones_row = jnp.where((lane64 >= nb) & (lane64 < nb + 3), 1.0, 0.0)
    out_lane = lax.broadcasted_iota(jnp.int32, (rows_per, 2 * hd), 1)

    def tile(c):
        q_augs = []
        for hh in range(2):
            lanes = slice(hh * hd, (hh + 1) * hd)
            qh = q_ref[:, lanes]
            g_t = _dot_nt(km_ref[0, :, lanes], qh.astype(F32), precision=lax.Precision.HIGHEST)
            rank = jnp.zeros((nb, blk), jnp.int32)
            for mblk in range(c):
                gm = g_t[mblk:mblk + 1, :]
                beats = (gm > g_t) | ((gm == g_t) & (mblk < n_iota))
                rank = rank + jnp.where(beats, 1, 0)
            keep = ((n_iota < c) & (rank < MOBA_TOPK)) | (n_iota == c)
            selb = jnp.where(keep, 0.0, NEG_INF)
            selb_t = jnp.concatenate([selb, jnp.zeros((128 - nb, blk), F32)], axis=0).T
            q_bias = (selb_t[:, :hd] + ones_row).astype(BF16)
            q_augs.append(jnp.concatenate([qh, q_bias], axis=1))
        chains = [(hh, rc) for rc in range(blk // rows_per) for hh in range(2)]
        mx = {}
        for hh, rc in chains:
            qa = q_augs[hh][rc * rows_per:(rc + 1) * rows_per]
            for n in range(c + 1):
                s = _dot_nt(qa, kaug_sc[n * blk:(n + 1) * blk, 2 * hh * hd:(2 * hh + 2) * hd])
                if n == c:
                    s = jnp.where(ri + rc * rows_per >= cj, s, NEG_INF)
                s_scs[hh][rc, n] = s
                t = jnp.maximum(s[:, :128], s[:, 128:])
                mx[hh, rc] = t if n == 0 else jnp.maximum(mx[hh, rc], t)
        m = {key: jnp.max(val, axis=-1, keepdims=True) for key, val in mx.items()}
        lsum, acc = {}, {}
        for hh, rc in chains:
            for n in range(c + 1):
                p = jnp.exp(s_scs[hh][rc, n] - m[hh, rc])
                t = p[:, :128] + p[:, 128:]
                pv = _dot(p.astype(BF16), v_ref[n * blk:(n + 1) * blk, :])
                lsum[hh, rc] = t if n == 0 else lsum[hh, rc] + t
                acc[hh, rc] = pv if n == 0 else acc[hh, rc] + pv
        for rc in range(blk // rows_per):
            o0, o1 = (acc[hh, rc] / jnp.sum(lsum[hh, rc], axis=-1, keepdims=True) for hh in range(2))
            o_ref[rc * rows_per:(rc + 1) * rows_per, :] = jnp.where(out_lane < hd, o0, o1).astype(BF16)

    for c in range(nb):
        pl.when(qi == c)(functools.partial(tile, c))


def _moba_attention(qb, kb, vb, kmean, B, S):
    nb = S // MOBA_BLOCK
    blk = MOBA_BLOCK
    hpairs = MOBA_HEADS // 2
    pair = 2 * HEAD_DIM
    tab = _moba_bias_table(S, nb)
    return pl.pallas_call(
        functools.partial(_moba_kernel, nb=nb),
        grid=(B, hpairs, nb),
        in_specs=[pl.BlockSpec((blk, pair), lambda b, h, q: (b * nb + q, h)),
                  pl.BlockSpec((S, pair), lambda b, h, q: (b, h)),
                  pl.BlockSpec((S, pair), lambda b, h, q: (b, h)),
                  pl.BlockSpec((1, nb, pair), lambda b, h, q: (b, 0, h)),
                  pl.BlockSpec((S, pair), lambda b, h, q: (0, h))],
        out_specs=pl.BlockSpec((blk, pair), lambda b, h, q: (b * nb + q, h)),
        out_shape=jax.ShapeDtypeStruct((B * S, MOBA_W), BF16),
        scratch_shapes=[pltpu.VMEM((S, 2 * pair), BF16),
                        pltpu.VMEM((blk // MOBA_ROWS, nb, MOBA_ROWS, blk), F32),
                        pltpu.VMEM((blk // MOBA_ROWS, nb, MOBA_ROWS, blk), F32)],
        compiler_params=pltpu.CompilerParams(dimension_semantics=("parallel", "parallel", "arbitrary"),
                                             vmem_limit_bytes=VMEM_LIMIT_BYTES),
        name="moba_attention",
    )(qb, kb, vb, kmean, tab)


def _route(scores_t, biased, tm):
    gsz = N_EXPERTS // N_GROUPS
    sub = lax.broadcasted_iota(jnp.int32, (gsz, tm), 0)
    grp = [biased[g * gsz:(g + 1) * gsz, :] for g in range(N_GROUPS)]
    gscore = []
    for v in grp:
        m1 = jnp.max(v, axis=0, keepdims=True)
        first = jnp.min(jnp.where(v == m1, sub, gsz), axis=0, keepdims=True)
        m2 = jnp.max(jnp.where(sub == first, REMOVED, v), axis=0, keepdims=True)
        gscore.append(m1 + m2)
    masked = []
    for g in range(N_GROUPS):
        rank = jnp.zeros((1, tm), jnp.int32)
        for g2 in range(N_GROUPS):
            if g2 == g:
                continue
            beats = (gscore[g2] >= gscore[g]) if g2 < g else (gscore[g2] > gscore[g])
            rank = rank + jnp.where(beats, 1, 0)
        masked.append(jnp.where(rank < TOPK_GROUPS, grp[g], NEG_INF))
    v = jnp.concatenate(masked, axis=0)
    e_iota = lax.broadcasted_iota(jnp.int32, (N_EXPERTS, tm), 0)
    picked = jnp.zeros((N_EXPERTS, tm), F32)
    experts, svals = [], []
    for _ in range(TOP_K):
        m = jnp.max(v, axis=0, keepdims=True)
        first = jnp.min(jnp.where(v == m, e_iota, N_EXPERTS), axis=0, keepdims=True)
        pick = e_iota == first
        experts.append(first)
        svals.append(jnp.sum(jnp.where(pick, scores_t, 0.0), axis=0, keepdims=True))
        picked = jnp.where(pick, 1.0, picked)
        v = jnp.where(pick, REMOVED, v)
    denom = svals[0]
    for s in svals[1:]:
        denom = denom + s
    weights = [s / denom * ROUTED_SCALE for s in svals]
    return experts, weights, picked


MOE_TILE = 2048
ROUTE_SHIFT = 12
assert MOE_TILE <= (1 << ROUTE_SHIFT)


def _mix_kernel(x_ref, aa_ref, ab_ref, wg_ref, wa_ref, wb_ref, wo_ref, g_ref, b_ref, wr_ref, rb_ref, tri_ref,
                h_ref, code_ref, wk_ref, cnt_ref, run_sc, *, alpha, tm):
    step = pl.program_id(0) % (MOE_TILE // tm)
    x = x_ref[...]
    xb = x.astype(BF16)
    d = x.shape[-1]
    ga = _dot(xb, wg_ref[:, :d])
    gb = _dot(xb, wg_ref[:, d:])
    y = _sigmoid(ga) * _dot(aa_ref[...], wa_ref[...]) + _sigmoid(gb) * _dot(ab_ref[...], wb_ref[...])
    mix = _dot(y.astype(BF16), wo_ref[...])
    h = _layernorm(alpha * x + mix, g_ref[...], b_ref[...])
    for j in range(d // LANES):
        h_ref[pl.ds(j, tm, stride=d // LANES), :] = h[:, j * LANES:(j + 1) * LANES]
    logits_t = _dot_nt(wr_ref[...], h, precision=lax.Precision.HIGHEST)
    scores_t = _sigmoid(logits_t)
    experts, weights, picked = _route(scores_t, scores_t + rb_ref[...], tm)

    @pl.when(step == 0)
    def _():
        run_sc[...] = jnp.zeros_like(run_sc)

    pos = _dot(picked.astype(BF16), tri_ref[...]) + run_sc[...]
    run_sc[...] += jnp.sum(picked, axis=1, keepdims=True)
    e_iota = lax.broadcasted_iota(jnp.int32, (N_EXPERTS, tm), 0)
    codes = []
    for k in range(TOP_K):
        rank = jnp.sum(jnp.where(e_iota == experts[k], pos, 0.0), axis=0, keepdims=True)
        codes.append(experts[k] * (1 << ROUTE_SHIFT) + rank.astype(jnp.int32))
    for c in range(tm // LANES):
        lanes = slice(c * LANES, (c + 1) * LANES)
        code_ref[0, c] = jnp.concatenate([v[:, lanes] for v in codes], axis=0)
        wk_ref[0, c] = jnp.concatenate([v[:, lanes] for v in weights], axis=0)

    picked_t = jnp.concatenate([picked, jnp.zeros((128 - N_EXPERTS, tm), F32)], axis=0).T
    counts = jnp.sum(picked_t, axis=0, keepdims=True).astype(jnp.int32)

    @pl.when(step == 0)
    def _():
        cnt_ref[0] = counts

    @pl.when(step != 0)
    def _():
        cnt_ref[0] += counts


def _mix_ln_route(xf, att_a, att_b, w_gates, w_a, w_b, w_o, ln_g, ln_b, w_router_t, router_bias, alpha, tm=512):
    T, D = xf.shape
    assert MOE_TILE % tm == 0 and T % MOE_TILE == 0
    steps = MOE_TILE // tm
    tri = jnp.asarray(np.triu(np.ones((tm, tm), np.float32), k=1), BF16)
    row = lambda w: pl.BlockSpec((tm, w), lambda i: (i, 0))
    col = pl.BlockSpec((1, tm // LANES, TOP_K, LANES), lambda i: (i // steps, i % steps, 0, 0))
    rec_shape = (T // MOE_TILE, MOE_TILE // LANES, TOP_K, LANES)
    full = lambda a: pl.BlockSpec(a.shape, lambda i: (0,) * a.ndim)
    return pl.pallas_call(
        functools.partial(_mix_kernel, alpha=alpha, tm=tm),
        grid=(T // tm,),
        in_specs=[row(D), row(SWA_Q_W), row(MOBA_W), full(w_gates), full(w_a), full(w_b), full(w_o),
                  full(ln_g), full(ln_b), full(w_router_t), full(router_bias), full(tri)],
        out_specs=[pl.BlockSpec((tm * (D // LANES), LANES), lambda i: (i, 0)), col, col, pl.BlockSpec((1, 1, 128), lambda i: (i // steps, 0, 0))],
        out_shape=[jax.ShapeDtypeStruct((T * (D // LANES), LANES), F32), jax.ShapeDtypeStruct(rec_shape, jnp.int32),
                   jax.ShapeDtypeStruct(rec_shape, F32), jax.ShapeDtypeStruct((T // MOE_TILE, 1, 128), jnp.int32)],
        scratch_shapes=[pltpu.VMEM((N_EXPERTS, 1), F32)],
        compiler_params=pltpu.CompilerParams(dimension_semantics=("arbitrary",), vmem_limit_bytes=VMEM_LIMIT_BYTES),
        name="mix_ln_route",
    )(xf, att_a, att_b, w_gates, w_a, w_b, w_o, ln_g, ln_b, w_router_t, router_bias, tri)


XY_STRIDE = MOE_TILE + 8
MOE_ROWS = 64
MOE_STATIC_BLOCKS = 12
MOE_PREFETCH_ROWS = 256
EDGE_ROWS = 256
GATHER_UNROLL = 16
SCATTER_UNROLL = 8


def _moe_kernel(h_ref, code_ref, wk_ref, cnt_ref, eg_ref, eu_ref, ed_ref, sg_ref, su_ref, sd_ref, g_ref, b_ref,
                o_ref, acc_sc, xy_a, xy_b, dst_sc, dst_s, wk_s, cnt_s, off_s, pick_s, sem, *, alpha, ch):
    tile = pl.program_id(0)
    pair = pl.program_id(1)
    tt = MOE_TILE

    def token_major_rows(ref, r0, rows):
        start = pl.multiple_of(r0 * ch, ch)
        return jnp.concatenate([ref[pl.ds(start + j, rows, stride=ch), :] for j in range(ch)], axis=1)

    def swiglu(xb, wg, wu, wd):
        a = _dot(xb, wg)
        return _dot((a * _sigmoid(a) * _dot(xb, wu)).astype(BF16), wd)

    n_picks = tt * TOP_K
    rec_rows = n_picks // LANES
    rows_per_expert = rec_rows // N_EXPERTS
    slot = tile % 2

    @pl.when((tile == 0) & (pair == 0))
    def _():
        xy_a[...] = jnp.zeros_like(xy_a)
        xy_b[...] = jnp.zeros_like(xy_b)
        for u in range(MOE_PREFETCH_ROWS):
            pick_s[2 * n_picks + u] = 0

    def prepare(j, slot_j):
        copies = [pltpu.make_async_copy(cnt_ref.at[j], cnt_s, sem.at[2]),
                  pltpu.make_async_copy(code_ref.at[j], dst_sc, sem.at[0])]
        for c in copies:
            c.start()
        for c in copies:
            c.wait()
        total = jnp.int32(0)
        for x in range(N_EXPERTS):
            off_s[slot_j * LANES + x] = total
            total = total + cnt_s[0, x]
        off_s[slot_j * LANES + N_EXPERTS] = total
        code = dst_sc[...]
        expert = code >> ROUTE_SHIFT
        dst = code & ((1 << ROUTE_SHIFT) - 1)
        for x in range(1, N_EXPERTS):
            dst = dst + jnp.where(expert == x, off_s[slot_j * LANES + x], 0)
        dst_sc[...] = dst
        dst_copy = pltpu.make_async_copy(dst_sc, dst_s, sem.at[0])
        dst_copy.start()
        dst_copy.wait()

    def build_rows(r0, rows, slot_j):
        for rr in range(rows):
            r = r0 + rr
            for u in range(LANES):
                pick_s[slot_j * n_picks + dst_s[r, u]] = r * LANES + u

    @pl.when(pair == 0)
    def _():
        wk_copy = pltpu.make_async_copy(wk_ref.at[tile], wk_s, sem.at[1])
        wk_copy.start()

        @pl.when(tile == 0)
        def _():
            prepare(0, 0)
            lax.fori_loop(0, rec_rows, lambda r, c: (build_rows(r, 1, 0), c)[1], 0)

        prepare(jnp.minimum(tile + 1, pl.num_programs(0) - 1), 1 - slot)
        wk_copy.wait()

        def shared_block(rb, carry):
            r0 = rb * EDGE_ROWS
            y = swiglu(token_major_rows(h_ref, r0, EDGE_ROWS).astype(BF16), sg_ref[...], su_ref[...], sd_ref[...])
            start = pl.multiple_of(r0 * ch, ch)
            for j in range(ch):
                acc_sc[pl.ds(start + j, EDGE_ROWS, stride=ch), :] = y[:, j * LANES:(j + 1) * LANES]
            return carry

        lax.fori_loop(0, tt // EDGE_ROWS, shared_block, 0)

    def expert_list(x):
        first = off_s[slot * LANES + x]
        return slot * n_picks + first, off_s[slot * LANES + x + 1] - first

    def token_rows(p):
        return pl.multiple_of((p >> 3) << 3, ch)

    def gather(buf, first, m0, count):
        for u in range(count):
            m = m0 + u
            buf[pl.ds(m, ch, stride=XY_STRIDE), :] = h_ref[pl.ds(token_rows(pick_s[first + m]), ch), :]

    def gather_range(buf, first, start, stop):
        groups = (stop - start) // GATHER_UNROLL
        lax.fori_loop(0, groups, lambda g, c: (gather(buf, first, start + g * GATHER_UNROLL, GATHER_UNROLL), c)[1], 0)
        lax.fori_loop(start + groups * GATHER_UNROLL, stop, lambda m, c: (gather(buf, first, m, 1), c)[1], 0)

    def run_expert(e, w, cur, nxt):
        pbase, n = expert_list(e)
        pbase_next, n_next = expert_list(jnp.minimum(e + 1, N_EXPERTS - 1))

        def expert_rows(r0, rows):
            xb = jnp.concatenate([cur[pl.ds(j * XY_STRIDE + r0, rows), :] for j in range(ch)], axis=1).astype(BF16)
            y = swiglu(xb, eg_ref[w], eu_ref[w], ed_ref[w])
            for j in range(ch):
                cur[pl.ds(j * XY_STRIDE + r0, rows), :] = y[:, j * LANES:(j + 1) * LANES]

        n_blocks = (n + MOE_ROWS - 1) // MOE_ROWS

        def expert_static(v):
            build_rows(e * rows_per_expert, rows_per_expert, 1 - slot)
            gather(nxt, pbase_next, 0, min(v * MOE_ROWS, MOE_PREFETCH_ROWS))
            if v > 0:
                expert_rows(0, v * MOE_ROWS)

        for v in range(MOE_STATIC_BLOCKS + 1):
            pl.when(n_blocks == v)(functools.partial(expert_static, v))

        @pl.when(n_blocks > MOE_STATIC_BLOCKS)
        def _():
            build_rows(e * rows_per_expert, rows_per_expert, 1 - slot)

            def expert_block(rb, carry):
                expert_rows(pl.multiple_of(rb * MOE_ROWS, MOE_ROWS), MOE_ROWS)
                return carry

            lax.fori_loop(0, n_blocks, expert_block, 0)

        staged = jnp.where(n_blocks > MOE_STATIC_BLOCKS, 0, jnp.minimum(n_blocks * MOE_ROWS, MOE_PREFETCH_ROWS))
        gather_range(nxt, pbase_next, jnp.minimum(staged, n_next), n_next)

        def scatter(m0, count):
            new = []
            for u in range(count):
                m = m0 + u
                p = pick_s[pbase + m]
                t8 = token_rows(p)
                y = cur[pl.ds(m, ch, stride=XY_STRIDE), :]
                new.append((t8, acc_sc[pl.ds(t8, ch), :] + wk_s[p] * y))
            for t8, v in new:
                acc_sc[pl.ds(t8, ch), :] = v

        n_main = (n // SCATTER_UNROLL) * SCATTER_UNROLL
        lax.fori_loop(0, n // SCATTER_UNROLL, lambda g, c: (scatter(g * SCATTER_UNROLL, SCATTER_UNROLL), c)[1], 0)
        lax.fori_loop(n_main, n, lambda m, c: (scatter(m, 1), c)[1], 0)

    @pl.when(pair == 0)
    def _():
        first, count = expert_list(0)
        gather_range(xy_a, first, 0, count)

    run_expert(2 * pair, 0, xy_a, xy_b)
    run_expert(2 * pair + 1, 1, xy_b, xy_a)

    @pl.when(pair == pl.num_programs(1) - 1)
    def _():
        def final_block(rb, carry):
            r0 = rb * EDGE_ROWS
            z = alpha * token_major_rows(h_ref, r0, EDGE_ROWS) + token_major_rows(acc_sc, r0, EDGE_ROWS)
            o_ref[pl.ds(pl.multiple_of(r0, EDGE_ROWS), EDGE_ROWS), :] = _layernorm(z, g_ref[...], b_ref[...])
            return carry

        lax.fori_loop(0, tt // EDGE_ROWS, final_block, 0)


def _moe(h_tm, code, wk, cnt, eg, eu, ed, sg, su, sd, ln_g, ln_b, alpha):
    E, D, F = eg.shape
    ch = D // LANES
    T = h_tm.shape[0] // ch
    assert ch == 8 and TOP_K == 8, "a token fills one 8-sublane group; pick ids are token * 8 + round"
    tt = MOE_TILE
    nt = T // tt
    rec_rows = tt * TOP_K // LANES
    code = code.transpose(0, 1, 3, 2).reshape(nt, rec_rows, LANES)
    wk = wk.transpose(0, 1, 3, 2).reshape(nt, tt * TOP_K)
    full = lambda a: pl.BlockSpec(a.shape, lambda i, e: (0,) * a.ndim)
    return pl.pallas_call(
        functools.partial(_moe_kernel, alpha=alpha, ch=ch),
        grid=(T // tt, E // 2),
        in_specs=[pl.BlockSpec((tt * ch, LANES), lambda i, e: (i, 0), pipeline_mode=pl.Buffered(1)),
                  pl.BlockSpec(memory_space=pl.ANY),
                  pl.BlockSpec(memory_space=pl.ANY),
                  pl.BlockSpec(memory_space=pl.ANY),
                  pl.BlockSpec((2, D, F), lambda i, e: (e, 0, 0)),
                  pl.BlockSpec((2, D, F), lambda i, e: (e, 0, 0)),
                  pl.BlockSpec((2, F, D), lambda i, e: (e, 0, 0)),
                  full(sg), full(su), full(sd), full(ln_g), full(ln_b)],
        out_specs=pl.BlockSpec((tt, D), lambda i, e: (i, 0), pipeline_mode=pl.Buffered(1)),
        out_shape=jax.ShapeDtypeStruct((T, D), F32),
        scratch_shapes=[pltpu.VMEM((tt * ch, LANES), F32),
                        pltpu.VMEM((ch * XY_STRIDE, LANES), F32),
                        pltpu.VMEM((ch * XY_STRIDE, LANES), F32),
                        pltpu.VMEM((rec_rows, LANES), jnp.int32),
                        pltpu.SMEM((rec_rows, LANES), jnp.int32),
                        pltpu.SMEM((TOP_K * tt,), F32),
                        pltpu.SMEM((1, LANES), jnp.int32),
                        pltpu.SMEM((2 * LANES,), jnp.int32),
                        pltpu.SMEM((2 * TOP_K * tt + MOE_PREFETCH_ROWS,), jnp.int32),
                        pltpu.SemaphoreType.DMA((3,))],
        compiler_params=pltpu.CompilerParams(dimension_semantics=("arbitrary", "arbitrary"),
                                             vmem_limit_bytes=VMEM_LIMIT_BYTES),
        name="moe_experts",
    )(h_tm, code, wk, cnt, eg, eu, ed, sg, su, sd, ln_g, ln_b)


def kernel(x, w_in, swa_sinks, w_branch_swa, w_branch_moba, w_out, ln1_g, ln1_b, w_router, router_bias,
           w_exp_gate, w_exp_up, w_exp_down, w_sh_gate, w_sh_up, w_sh_down, ln2_g, ln2_b):
    B, S, D = x.shape
    depth = w_in.shape[0]
    alpha = (2.0 * depth) ** 0.25
    h = x.reshape(B * S, D)
    for l in range(depth):
        w_in_b = w_in[l].astype(BF16)
        qa, ka, va, qb, kb, vb, kmean = _qkv_proj(h, w_in_b[:, :QKV_COLS])
        att_a = _swa_attention(qa, ka, va, swa_sinks[l], B, S)
        att_b = _moba_attention(qb, kb, vb, kmean.reshape(B, S // MOBA_BLOCK, MOBA_W), B, S)
        h1, code, wk, cnt = _mix_ln_route(
            h, att_a, att_b, w_in_b[:, QKV_COLS:], w_branch_swa[l].astype(BF16), w_branch_moba[l].astype(BF16),
            w_out[l].astype(BF16), ln1_g[l].reshape(1, D), ln1_b[l].reshape(1, D),
            w_router[l].T, router_bias[l].reshape(N_EXPERTS, 1), alpha)
        h = _moe(h1, code, wk, cnt, w_exp_gate[l].astype(BF16), w_exp_up[l].astype(BF16),
                 w_exp_down[l].astype(BF16), w_sh_gate[l].astype(BF16), w_sh_up[l].astype(BF16),
                 w_sh_down[l].astype(BF16), ln2_g[l].reshape(1, D), ln2_b[l].reshape(1, D), alpha)
    return h.reshape(B, S, D)
```

```python
import functools

import numpy as np
import jax
import jax.numpy as jnp
from jax import lax
from jax.experimental import pallas as pl
from jax.experimental.pallas import tpu as pltpu

F32 = jnp.float32
BF16 = jnp.bfloat16

HEAD_DIM = 64
SWA_HEADS = 8
SWA_KV_HEADS = 2
SWA_BLOCK = 128
MOBA_HEADS = 8
MOBA_BLOCK = 256
MOBA_TOPK = 3
N_EXPERTS = 64
TOP_K = 8
N_GROUPS = 8
TOPK_GROUPS = 4
ROUTED_SCALE = 2.5
LN_EPS = 1e-5
NEG_INF = -1e30
REMOVED = -3e38

SWA_Q_W = SWA_HEADS * HEAD_DIM
SWA_KV_W = SWA_KV_HEADS * HEAD_DIM
MOBA_W = MOBA_HEADS * HEAD_DIM
QKV_COLS = SWA_Q_W + 2 * SWA_KV_W + 3 * MOBA_W

_N_SOFTMAX_HEADS = SWA_HEADS + MOBA_HEADS
_SLOPES = np.asarray(2.0 ** (-8.0 * np.arange(1, _N_SOFTMAX_HEADS + 1) / _N_SOFTMAX_HEADS), np.float32)
SWA_SLOPES = [float(s) for s in _SLOPES[:SWA_HEADS]]
MOBA_SLOPES = [float(s) for s in _SLOPES[SWA_HEADS:]]

LANES = 128
VMEM_LIMIT_BYTES = 56 * 1024 * 1024


def _dot(a, b):
    return jnp.dot(a, b, preferred_element_type=F32)


def _dot_nt(a, b, precision=None):
    return lax.dot_general(a, b, (((1,), (1,)), ((), ())), preferred_element_type=F32, precision=precision)


def _sigmoid(x):
    return 1.0 / (1.0 + jnp.exp(-x))


def _layernorm(z, g, b):
    mu = jnp.mean(z, axis=-1, keepdims=True)
    zc = z - mu
    var = jnp.mean(zc * zc, axis=-1, keepdims=True)
    return zc * lax.rsqrt(var + LN_EPS) * g + b


def _qkv_kernel(x_ref, w_ref, qa_ref, ka_ref, va_ref, qb_ref, kb_ref, vb_ref, km_ref, *, tm):
    xb = x_ref[...].astype(BF16)
    scale = HEAD_DIM ** -0.5

    def proj(lo, hi):
        return _dot(xb, w_ref[:, lo:hi])

    c = 0
    qa_ref[...] = (proj(c, c + SWA_Q_W) * scale).astype(BF16)
    c += SWA_Q_W
    ka_ref[...] = proj(c, c + SWA_KV_W).astype(BF16)
    c += SWA_KV_W
    va_ref[...] = proj(c, c + SWA_KV_W).astype(BF16)
    c += SWA_KV_W
    qb_ref[...] = (proj(c, c + MOBA_W) * scale).astype(BF16)
    c += MOBA_W
    kb = proj(c, c + MOBA_W)
    kb_ref[...] = kb.astype(BF16)
    for i in range(tm // MOBA_BLOCK):
        blk = kb[i * MOBA_BLOCK:(i + 1) * MOBA_BLOCK, :]
        km_ref[0, i:i + 1, :] = jnp.sum(blk, axis=0, keepdims=True) * (1.0 / MOBA_BLOCK)
    c += MOBA_W
    vb_ref[...] = proj(c, c + MOBA_W).astype(BF16)


def _qkv_proj(xf, w_qkv, tm=512):
    T, D = xf.shape
    nt = T // tm
    row = lambda w: pl.BlockSpec((tm, w), lambda i: (i, 0))
    outs = pl.pallas_call(
        functools.partial(_qkv_kernel, tm=tm),
        grid=(nt,),
        in_specs=[row(D), pl.BlockSpec((D, QKV_COLS), lambda i: (0, 0))],
        out_specs=[row(SWA_Q_W), row(SWA_KV_W), row(SWA_KV_W), row(MOBA_W), row(MOBA_W), row(MOBA_W),
                   pl.BlockSpec((1, tm // MOBA_BLOCK, MOBA_W), lambda i: (i, 0, 0))],
        out_shape=[jax.ShapeDtypeStruct((T, SWA_Q_W), BF16), jax.ShapeDtypeStruct((T, SWA_KV_W), BF16),
                   jax.ShapeDtypeStruct((T, SWA_KV_W), BF16), jax.ShapeDtypeStruct((T, MOBA_W), BF16),
                   jax.ShapeDtypeStruct((T, MOBA_W), BF16), jax.ShapeDtypeStruct((T, MOBA_W), BF16),
                   jax.ShapeDtypeStruct((nt, tm // MOBA_BLOCK, MOBA_W), F32)],
        compiler_params=pltpu.CompilerParams(dimension_semantics=("parallel",), vmem_limit_bytes=VMEM_LIMIT_BYTES),
        name="qkv_proj",
    )(xf, w_qkv)
    return outs


def _swa_bias_table():
    blk = SWA_BLOCK
    i = np.arange(blk)[:, None]
    j = np.arange(2 * blk)[None, :]
    dist = i + blk - j
    band = (dist >= 0) & (dist < blk)
    alibi = -(np.asarray(SWA_SLOPES, np.float32)[:, None, None] * dist.astype(np.float32)[None])
    tabs = [np.where((band & (j >= blk))[None], alibi, np.float32(NEG_INF)),
            np.where(band[None], alibi, np.float32(NEG_INF))]
    return jnp.asarray(np.stack(tabs).astype(np.float32))


def _swa_kernel(sink_ref, q_ref, kp_ref, ko_ref, vp_ref, vo_ref, bias_ref, o_ref):
    blk = SWA_BLOCK
    k = jnp.concatenate([kp_ref[...], ko_ref[...]], axis=0)
    v = jnp.concatenate([vp_ref[...], vo_ref[...]], axis=0)
    group = SWA_HEADS // SWA_KV_HEADS
    heads = range(SWA_HEADS)
    kvs = [h // group for h in heads]
    s = [_dot_nt(q_ref[:, h * HEAD_DIM:(h + 1) * HEAD_DIM], k[:, kvs[h] * HEAD_DIM:(kvs[h] + 1) * HEAD_DIM])
         + bias_ref[0, h] for h in heads]
    m = [jnp.maximum(jnp.max(jnp.maximum(s[h][:, :blk], s[h][:, blk:]), axis=-1, keepdims=True), sink_ref[h])
         for h in heads]
    e = [jnp.exp(s[h] - m[h]) for h in heads]
    denom = [jnp.sum(e[h][:, :blk] + e[h][:, blk:], axis=-1, keepdims=True) + jnp.exp(sink_ref[h] - m[h])
             for h in heads]
    outs = [_dot(e[h].astype(BF16), v[:, kvs[h] * HEAD_DIM:(kvs[h] + 1) * HEAD_DIM]) / denom[h] for h in heads]
    o_ref[...] = jnp.concatenate(outs, axis=-1).astype(BF16)


def _swa_attention(qa, ka, va, sinks, B, S):
    nq = S // SWA_BLOCK
    own = lambda b, n: (b * nq + n, 0)
    prev = lambda b, n: (b * nq + jnp.maximum(n - 1, 0), 0)
    kv_spec = lambda im: pl.BlockSpec((SWA_BLOCK, SWA_KV_W), im)
    return pl.pallas_call(
        _swa_kernel,
        grid=(B, nq),
        in_specs=[pl.BlockSpec(memory_space=pltpu.SMEM),
                  pl.BlockSpec((SWA_BLOCK, SWA_Q_W), own),
                  kv_spec(prev), kv_spec(own), kv_spec(prev), kv_spec(own),
                  pl.BlockSpec((1, SWA_HEADS, SWA_BLOCK, 2 * SWA_BLOCK), lambda b, n: (jnp.minimum(n, 1), 0, 0, 0))],
        out_specs=pl.BlockSpec((SWA_BLOCK, SWA_Q_W), own),
        out_shape=jax.ShapeDtypeStruct((B * S, SWA_Q_W), BF16),
        compiler_params=pltpu.CompilerParams(dimension_semantics=("parallel", "parallel")),
        name="swa_attention",
    )(sinks, qa, ka, ka, va, va, _swa_bias_table())


MOBA_ROWS = 256


def _moba_bias_table(S, nb):
    pos = np.arange(S)
    onehot = np.zeros((S, MOBA_HEADS, HEAD_DIM), np.float32)
    onehot[pos, :, pos // MOBA_BLOCK] = 1.0
    tab = jnp.asarray(onehot)
    rem = jnp.asarray(np.asarray(MOBA_SLOPES, np.float32)[None, :] * pos[:, None].astype(np.float32))
    for t in range(3):
        part = rem.astype(BF16).astype(F32)
        tab = tab.at[:, :, nb + t].set(part)
        rem = rem - part
    return tab.reshape(S, MOBA_HEADS * HEAD_DIM).astype(BF16)


def _moba_kernel(q_ref, k_ref, v_ref, km_ref, tab_ref, o_ref, kaug_sc, s_sc0, s_sc1, *, nb):
    s_scs = (s_sc0, s_sc1)
    qi = pl.program_id(2)
    blk = MOBA_BLOCK
    hd = HEAD_DIM
    rows_per = MOBA_ROWS

    @pl.when(qi == 0)
    def _():
        for hh in range(2):
            kaug_sc[:, 2 * hh * hd:(2 * hh + 1) * hd] = k_ref[:, hh * hd:(hh + 1) * hd]
            kaug_sc[:, (2 * hh + 1) * hd:(2 * hh + 2) * hd] = tab_ref[:, hh * hd:(hh + 1) * hd]

    ri = lax.broadcasted_iota(jnp.int32, (rows_per, blk), 0)
    cj = lax.broadcasted_iota(jnp.int32, (rows_per, blk), 1)
    n_iota = lax.broadcasted_iota(jnp.int32, (nb, blk), 0)
    lane64 = lax.broadcasted_iota(jnp.int32, (1, hd), 1)
    ones_row = jnp.where((lane64 >= nb) & (lane64 < nb + 3), 1.0, 0.0)
    out_lane = lax.broadcasted_iota(jnp.int32, (rows_per, 2 * hd), 1)

    def tile(c):
        q_augs = []
        for hh in range(2):
            lanes = slice(hh * hd, (hh + 1) * hd)
            qh = q_ref[:, lanes]
            g_t = _dot_nt(km_ref[0, :, lanes], qh.astype(F32), precision=lax.Precision.HIGHEST)
            rank = jnp.zeros((nb, blk), jnp.int32)
            for mblk in range(c):
                gm = g_t[mblk:mblk + 1, :]
                beats = (gm > g_t) | ((gm == g_t) & (mblk < n_iota))
                rank = rank + jnp.where(beats, 1, 0)
            keep = ((n_iota < c) & (rank < MOBA_TOPK)) | (n_iota == c)
            selb = jnp.where(keep, 0.0, NEG_INF)
            selb_t = jnp.concatenate([selb, jnp.zeros((128 - nb, blk), F32)], axis=0).T
            q_bias = (selb_t[:, :hd] + ones_row).astype(BF16)
            q_augs.append(jnp.concatenate([qh, q_bias], axis=1))
        chains = [(hh, rc) for rc in range(blk // rows_per) for hh in range(2)]
        mx = {}
        for hh, rc in chains:
            qa = q_augs[hh][rc * rows_per:(rc + 1) * rows_per]
            for n in range(c + 1):
                s = _dot_nt(qa, kaug_sc[n * blk:(n + 1) * blk, 2 * hh * hd:(2 * hh + 2) * hd])
                if n == c:
                    s = jnp.where(ri + rc * rows_per >= cj, s, NEG_INF)
                s_scs[hh][rc, n] = s
                t = jnp.maximum(s[:, :128], s[:, 128:])
                mx[hh, rc] = t if n == 0 else jnp.maximum(mx[hh, rc], t)
        m = {key: jnp.max(val, axis=-1, keepdims=True) for key, val in mx.items()}
        lsum, acc = {}, {}
        for hh, rc in chains:
            for n in range(c + 1):
                p = jnp.exp(s_scs[hh][rc, n] - m[hh, rc])
                t = p[:, :128] + p[:, 128:]
                pv = _dot(p.astype(BF16), v_ref[n * blk:(n + 1) * blk, :])
                lsum[hh, rc] = t if n == 0 else lsum[hh, rc] + t
                acc[hh, rc] = pv if n == 0 else acc[hh, rc] + pv
        for rc in range(blk // rows_per):
            o0, o1 = (acc[hh, rc] / jnp.sum(lsum[hh, rc], axis=-1, keepdims=True) for hh in range(2))
            o_ref[rc * rows_per:(rc + 1) * rows_per, :] = jnp.where(out_lane < hd, o0, o1).astype(BF16)

    for c in range(nb):
        pl.when(qi == c)(functools.partial(tile, c))


def _moba_attention(qb, kb, vb, kmean, B, S):
    nb = S // MOBA_BLOCK
    blk = MOBA_BLOCK
    hpairs = MOBA_HEADS // 2
    pair = 2 * HEAD_DIM
    tab = _moba_bias_table(S, nb)
    return pl.pallas_call(
        functools.partial(_moba_kernel, nb=nb),
        grid=(B, hpairs, nb),
        in_specs=[pl.BlockSpec((blk, pair), lambda b, h, q: (b * nb + q, h)),
                  pl.BlockSpec((S, pair), lambda b, h, q: (b, h)),
                  pl.BlockSpec((S, pair), lambda b, h, q: (b, h)),
                  pl.BlockSpec((1, nb, pair), lambda b, h, q: (b, 0, h)),
                  pl.BlockSpec((S, pair), lambda b, h, q: (0, h))],
        out_specs=pl.BlockSpec((blk, pair), lambda b, h, q: (b * nb + q, h)),
        out_shape=jax.ShapeDtypeStruct((B * S, MOBA_W), BF16),
        scratch_shapes=[pltpu.VMEM((S, 2 * pair), BF16),
                        pltpu.VMEM((blk // MOBA_ROWS, nb, MOBA_ROWS, blk), F32),
                        pltpu.VMEM((blk // MOBA_ROWS, nb, MOBA_ROWS, blk), F32)],
        compiler_params=pltpu.CompilerParams(dimension_semantics=("parallel", "parallel", "arbitrary"),
                                             vmem_limit_bytes=VMEM_LIMIT_BYTES),
        name="moba_attention",
    )(qb, kb, vb, kmean, tab)


def _route(scores_t, biased, tm):
    gsz = N_EXPERTS // N_GROUPS
    sub = lax.broadcasted_iota(jnp.int32, (gsz, tm), 0)
    grp = [biased[g * gsz:(g + 1) * gsz, :] for g in range(N_GROUPS)]
    gscore = []
    for v in grp:
        m1 = jnp.max(v, axis=0, keepdims=True)
        first = jnp.min(jnp.where(v == m1, sub, gsz), axis=0, keepdims=True)
        m2 = jnp.max(jnp.where(sub == first, REMOVED, v), axis=0, keepdims=True)
        gscore.append(m1 + m2)
    masked = []
    for g in range(N_GROUPS):
        rank = jnp.zeros((1, tm), jnp.int32)
        for g2 in range(N_GROUPS):
            if g2 == g:
                continue
            beats = (gscore[g2] >= gscore[g]) if g2 < g else (gscore[g2] > gscore[g])
            rank = rank + jnp.where(beats, 1, 0)
        masked.append(jnp.where(rank < TOPK_GROUPS, grp[g], NEG_INF))
    v = jnp.concatenate(masked, axis=0)
    e_iota = lax.broadcasted_iota(jnp.int32, (N_EXPERTS, tm), 0)
    picked = jnp.zeros((N_EXPERTS, tm), F32)
    experts, svals = [], []
    for _ in range(TOP_K):
        m = jnp.max(v, axis=0, keepdims=True)
        first = jnp.min(jnp.where(v == m, e_iota, N_EXPERTS), axis=0, keepdims=True)
        pick = e_iota == first
        experts.append(first)
        svals.append(jnp.sum(jnp.where(pick, scores_t, 0.0), axis=0, keepdims=True))
        picked = jnp.where(pick, 1.0, picked)
        v = jnp.where(pick, REMOVED, v)
    denom = svals[0]
    for s in svals[1:]:
        denom = denom + s
    weights = [s / denom * ROUTED_SCALE for s in svals]
    return experts, weights, picked


MOE_TILE = 2048
ROUTE_SHIFT = 12
assert MOE_TILE <= (1 << ROUTE_SHIFT)


def _mix_kernel(x_ref, aa_ref, ab_ref, wg_ref, wa_ref, wb_ref, wo_ref, g_ref, b_ref, wr_ref, rb_ref, tri_ref,
                h_ref, code_ref, wk_ref, cnt_ref, run_sc, *, alpha, tm):
    step = pl.program_id(0) % (MOE_TILE // tm)
    x = x_ref[...]
    xb = x.astype(BF16)
    d = x.shape[-1]
    ga = _dot(xb, wg_ref[:, :d])
    gb = _dot(xb, wg_ref[:, d:])
    y = _sigmoid(ga) * _dot(aa_ref[...], wa_ref[...]) + _sigmoid(gb) * _dot(ab_ref[...], wb_ref[...])
    mix = _dot(y.astype(BF16), wo_ref[...])
    h = _layernorm(alpha * x + mix, g_ref[...], b_ref[...])
    for j in range(d // LANES):
        h_ref[pl.ds(j, tm, stride=d // LANES), :] = h[:, j * LANES:(j + 1) * LANES]
    logits_t = _dot_nt(wr_ref[...], h, precision=lax.Precision.HIGHEST)
    scores_t = _sigmoid(logits_t)
    experts, weights, picked = _route(scores_t, scores_t + rb_ref[...], tm)

    @pl.when(step == 0)
    def _():
        run_sc[...] = jnp.zeros_like(run_sc)

    pos = _dot(picked.astype(BF16), tri_ref[...]) + run_sc[...]
    run_sc[...] += jnp.sum(picked, axis=1, keepdims=True)
    e_iota = lax.broadcasted_iota(jnp.int32, (N_EXPERTS, tm), 0)
    codes = []
    for k in range(TOP_K):
        rank = jnp.sum(jnp.where(e_iota == experts[k], pos, 0.0), axis=0, keepdims=True)
        codes.append(experts[k] * (1 << ROUTE_SHIFT) + rank.astype(jnp.int32))
    for c in range(tm // LANES):
        lanes = slice(c * LANES, (c + 1) * LANES)
        code_ref[0, c] = jnp.concatenate([v[:, lanes] for v in codes], axis=0)
        wk_ref[0, c] = jnp.concatenate([v[:, lanes] for v in weights], axis=0)

    picked_t = jnp.concatenate([picked, jnp.zeros((128 - N_EXPERTS, tm), F32)], axis=0).T
    counts = jnp.sum(picked_t, axis=0, keepdims=True).astype(jnp.int32)

    @pl.when(step == 0)
    def _():
        cnt_ref[0] = counts

    @pl.when(step != 0)
    def _():
        cnt_ref[0] += counts


def _mix_ln_route(xf, att_a, att_b, w_gates, w_a, w_b, w_o, ln_g, ln_b, w_router_t, router_bias, alpha, tm=512):
    T, D = xf.shape
    assert MOE_TILE % tm == 0 and T % MOE_TILE == 0
    steps = MOE_TILE // tm
    tri = jnp.asarray(np.triu(np.ones((tm, tm), np.float32), k=1), BF16)
    row = lambda w: pl.BlockSpec((tm, w), lambda i: (i, 0))
    col = pl.BlockSpec((1, tm // LANES, TOP_K, LANES), lambda i: (i // steps, i % steps, 0, 0))
    rec_shape = (T // MOE_TILE, MOE_TILE // LANES, TOP_K, LANES)
    full = lambda a: pl.BlockSpec(a.shape, lambda i: (0,) * a.ndim)
    return pl.pallas_call(
        functools.partial(_mix_kernel, alpha=alpha, tm=tm),
        grid=(T // tm,),
        in_specs=[row(D), row(SWA_Q_W), row(MOBA_W), full(w_gates), full(w_a), full(w_b), full(w_o),
                  full(ln_g), full(ln_b), full(w_router_t), full(router_bias), full(tri)],
        out_specs=[pl.BlockSpec((tm * (D // LANES), LANES), lambda i: (i, 0)), col, col, pl.BlockSpec((1, 1, 128), lambda i: (i // steps, 0, 0))],
        out_shape=[jax.ShapeDtypeStruct((T * (D // LANES), LANES), F32), jax.ShapeDtypeStruct(rec_shape, jnp.int32),
                   jax.ShapeDtypeStruct(rec_shape, F32), jax.ShapeDtypeStruct((T // MOE_TILE, 1, 128), jnp.int32)],
        scratch_shapes=[pltpu.VMEM((N_EXPERTS, 1), F32)],
        compiler_params=pltpu.CompilerParams(dimension_semantics=("arbitrary",), vmem_limit_bytes=VMEM_LIMIT_BYTES),
        name="mix_ln_route",
    )(xf, att_a, att_b, w_gates, w_a, w_b, w_o, ln_g, ln_b, w_router_t, router_bias, tri)


XY_STRIDE = MOE_TILE + 8
MOE_ROWS = 64
MOE_STATIC_BLOCKS = 12
MOE_PREFETCH_ROWS = 256
EDGE_ROWS = 256
GATHER_UNROLL = 16
SCATTER_UNROLL = 8


def _moe_kernel(h_ref, code_ref, wk_ref, cnt_ref, eg_ref, eu_ref, ed_ref, sg_ref, su_ref, sd_ref, g_ref, b_ref,
                o_ref, acc_sc, xy_a, xy_b, dst_sc, dst_s, wk_s, cnt_s, off_s, pick_s, sem, *, alpha, ch):
    tile = pl.program_id(0)
    pair = pl.program_id(1)
    tt = MOE_TILE

    def token_major_rows(ref, r0, rows):
        start = pl.multiple_of(r0 * ch, ch)
        return jnp.concatenate([ref[pl.ds(start + j, rows, stride=ch), :] for j in range(ch)], axis=1)

    def swiglu(xb, wg, wu, wd):
        a = _dot(xb, wg)
        return _dot((a * _sigmoid(a) * _dot(xb, wu)).astype(BF16), wd)

    n_picks = tt * TOP_K
    rec_rows = n_picks // LANES
    rows_per_expert = rec_rows // N_EXPERTS
    slot = tile % 2

    @pl.when((tile == 0) & (pair == 0))
    def _():
        xy_a[...] = jnp.zeros_like(xy_a)
        xy_b[...] = jnp.zeros_like(xy_b)
        for u in range(MOE_PREFETCH_ROWS):
            pick_s[2 * n_picks + u] = 0

    def prepare(j, slot_j):
        copies = [pltpu.make_async_copy(cnt_ref.at[j], cnt_s, sem.at[2]),
                  pltpu.make_async_copy(code_ref.at[j], dst_sc, sem.at[0])]
        for c in copies:
            c.start()
        for c in copies:
            c.wait()
        total = jnp.int32(0)
        for x in range(N_EXPERTS):
            off_s[slot_j * LANES + x] = total
            total = total + cnt_s[0, x]
        off_s[slot_j * LANES + N_EXPERTS] = total
        code = dst_sc[...]
        expert = code >> ROUTE_SHIFT
        dst = code & ((1 << ROUTE_SHIFT) - 1)
        for x in range(1, N_EXPERTS):
            dst = dst + jnp.where(expert == x, off_s[slot_j * LANES + x], 0)
        dst_sc[...] = dst
        dst_copy = pltpu.make_async_copy(dst_sc, dst_s, sem.at[0])
        dst_copy.start()
        dst_copy.wait()

    def build_rows(r0, rows, slot_j):
        for rr in range(rows):
            r = r0 + rr
            for u in range(LANES):
                pick_s[slot_j * n_picks + dst_s[r, u]] = r * LANES + u

    @pl.when(pair == 0)
    def _():
        wk_copy = pltpu.make_async_copy(wk_ref.at[tile], wk_s, sem.at[1])
        wk_copy.start()

        @pl.when(tile == 0)
        def _():
            prepare(0, 0)
            lax.fori_loop(0, rec_rows, lambda r, c: (build_rows(r, 1, 0), c)[1], 0)

        prepare(jnp.minimum(tile + 1, pl.num_programs(0) - 1), 1 - slot)
        wk_copy.wait()

        def shared_block(rb, carry):
            r0 = rb * EDGE_ROWS
            y = swiglu(token_major_rows(h_ref, r0, EDGE_ROWS).astype(BF16), sg_ref[...], su_ref[...], sd_ref[...])
            start = pl.multiple_of(r0 * ch, ch)
            for j in range(ch):
                acc_sc[pl.ds(start + j, EDGE_ROWS, stride=ch), :] = y[:, j * LANES:(j + 1) * LANES]
            return carry

        lax.fori_loop(0, tt // EDGE_ROWS, shared_block, 0)

    def expert_list(x):
        first = off_s[slot * LANES + x]
        return slot * n_picks + first, off_s[slot * LANES + x + 1] - first

    def token_rows(p):
        return pl.multiple_of(p & -8, ch)

    def gather(buf, first, m0, count):
        for u in range(count):
            m = m0 + u
            buf[pl.ds(m, ch, stride=XY_STRIDE), :] = h_ref[pl.ds(token_rows(pick_s[first + m]), ch), :]

    def gather_range(buf, first, start, stop):
        groups = (stop - start) // GATHER_UNROLL
        lax.fori_loop(0, groups, lambda g, c: (gather(buf, first, start + g * GATHER_UNROLL, GATHER_UNROLL), c)[1], 0)
        lax.fori_loop(start + groups * GATHER_UNROLL, stop, lambda m, c: (gather(buf, first, m, 1), c)[1], 0)

    def run_expert(e, w, cur, nxt):
        pbase, n = expert_list(e)
        pbase_next, n_next = expert_list(jnp.minimum(e + 1, N_EXPERTS - 1))

        def expert_rows(r0, rows):
            xb = jnp.concatenate([cur[pl.ds(j * XY_STRIDE + r0, rows), :] for j in range(ch)], axis=1).astype(BF16)
            y = swiglu(xb, eg_ref[w], eu_ref[w], ed_ref[w])
            for j in range(ch):
                cur[pl.ds(j * XY_STRIDE + r0, rows), :] = y[:, j * LANES:(j + 1) * LANES]

        n_blocks = (n + MOE_ROWS - 1) // MOE_ROWS

        def expert_static(v):
            build_rows(e * rows_per_expert, rows_per_expert, 1 - slot)
            gather(nxt, pbase_next, 0, min(v * MOE_ROWS, MOE_PREFETCH_ROWS))
            if v > 0:
                expert_rows(0, v * MOE_ROWS)

        for v in range(MOE_STATIC_BLOCKS + 1):
            pl.when(n_blocks == v)(functools.partial(expert_static, v))

        @pl.when(n_blocks > MOE_STATIC_BLOCKS)
        def _():
            build_rows(e * rows_per_expert, rows_per_expert, 1 - slot)

            def expert_block(rb, carry):
                expert_rows(pl.multiple_of(rb * MOE_ROWS, MOE_ROWS), MOE_ROWS)
                return carry

            lax.fori_loop(0, n_blocks, expert_block, 0)

        staged = jnp.where(n_blocks > MOE_STATIC_BLOCKS, 0, jnp.minimum(n_blocks * MOE_ROWS, MOE_PREFETCH_ROWS))
        gather_range(nxt, pbase_next, jnp.minimum(staged, n_next), n_next)

        def scatter(m0, count):
            new = []
            for u in range(count):
                m = m0 + u
                p = pick_s[pbase + m]
                t8 = token_rows(p)
                y = cur[pl.ds(m, ch, stride=XY_STRIDE), :]
                new.append((t8, acc_sc[pl.ds(t8, ch), :] + wk_s[p] * y))
            for t8, v in new:
                acc_sc[pl.ds(t8, ch), :] = v

        n_main = (n // SCATTER_UNROLL) * SCATTER_UNROLL
        lax.fori_loop(0, n // SCATTER_UNROLL, lambda g, c: (scatter(g * SCATTER_UNROLL, SCATTER_UNROLL), c)[1], 0)
        lax.fori_loop(n_main, n, lambda m, c: (scatter(m, 1), c)[1], 0)

    @pl.when(pair == 0)
    def _():
        first, count = expert_list(0)
        gather_range(xy_a, first, 0, count)

    run_expert(2 * pair, 0, xy_a, xy_b)
    run_expert(2 * pair + 1, 1, xy_b, xy_a)

    @pl.when(pair == pl.num_programs(1) - 1)
    def _():
        def final_block(rb, carry):
            r0 = rb * EDGE_ROWS
            z = alpha * token_major_rows(h_ref, r0, EDGE_ROWS) + token_major_rows(acc_sc, r0, EDGE_ROWS)
            o_ref[pl.ds(pl.multiple_of(r0, EDGE_ROWS), EDGE_ROWS), :] = _layernorm(z, g_ref[...], b_ref[...])
            return carry

        lax.fori_loop(0, tt // EDGE_ROWS, final_block, 0)


def _moe(h_tm, code, wk, cnt, eg, eu, ed, sg, su, sd, ln_g, ln_b, alpha):
    E, D, F = eg.shape
    ch = D // LANES
    T = h_tm.shape[0] // ch
    assert ch == 8 and TOP_K == 8, "a token fills one 8-sublane group; pick ids are token * 8 + round"
    tt = MOE_TILE
    nt = T // tt
    rec_rows = tt * TOP_K // LANES
    code = code.transpose(0, 1, 3, 2).reshape(nt, rec_rows, LANES)
    wk = wk.transpose(0, 1, 3, 2).reshape(nt, tt * TOP_K)
    full = lambda a: pl.BlockSpec(a.shape, lambda i, e: (0,) * a.ndim)
    return pl.pallas_call(
        functools.partial(_moe_kernel, alpha=alpha, ch=ch),
        grid=(T // tt, E // 2),
        in_specs=[pl.BlockSpec((tt * ch, LANES), lambda i, e: (i, 0), pipeline_mode=pl.Buffered(1)),
                  pl.BlockSpec(memory_space=pl.ANY),
                  pl.BlockSpec(memory_space=pl.ANY),
                  pl.BlockSpec(memory_space=pl.ANY),
                  pl.BlockSpec((2, D, F), lambda i, e: (e, 0, 0)),
                  pl.BlockSpec((2, D, F), lambda i, e: (e, 0, 0)),
                  pl.BlockSpec((2, F, D), lambda i, e: (e, 0, 0)),
                  full(sg), full(su), full(sd), full(ln_g), full(ln_b)],
        out_specs=pl.BlockSpec((tt, D), lambda i, e: (i, 0), pipeline_mode=pl.Buffered(1)),
        out_shape=jax.ShapeDtypeStruct((T, D), F32),
        scratch_shapes=[pltpu.VMEM((tt * ch, LANES), F32),
                        pltpu.VMEM((ch * XY_STRIDE, LANES), F32),
                        pltpu.VMEM((ch * XY_STRIDE, LANES), F32),
                        pltpu.VMEM((rec_rows, LANES), jnp.int32),
                        pltpu.SMEM((rec_rows, LANES), jnp.int32),
                        pltpu.SMEM((TOP_K * tt,), F32),
                        pltpu.SMEM((1, LANES), jnp.int32),
                        pltpu.SMEM((2 * LANES,), jnp.int32),
                        pltpu.SMEM((2 * TOP_K * tt + MOE_PREFETCH_ROWS,), jnp.int32),
                        pltpu.SemaphoreType.DMA((3,))],
        compiler_params=pltpu.CompilerParams(dimension_semantics=("arbitrary", "arbitrary"),
                                             vmem_limit_bytes=VMEM_LIMIT_BYTES),
        name="moe_experts",
    )(h_tm, code, wk, cnt, eg, eu, ed, sg, su, sd, ln_g, ln_b)


def kernel(x, w_in, swa_sinks, w_branch_swa, w_branch_moba, w_out, ln1_g, ln1_b, w_router, router_bias,
           w_exp_gate, w_exp_up, w_exp_down, w_sh_gate, w_sh_up, w_sh_down, ln2_g, ln2_b):
    B, S, D = x.shape
    depth = w_in.shape[0]
    alpha = (2.0 * depth) ** 0.25
    h = x.reshape(B * S, D)
    for l in range(depth):
        w_in_b = w_in[l].astype(BF16)
        qa, ka, va, qb, kb, vb, kmean = _qkv_proj(h, w_in_b[:, :QKV_COLS])
        att_a = _swa_attention(qa, ka, va, swa_sinks[l], B, S)
        att_b = _moba_attention(qb, kb, vb, kmean.reshape(B, S // MOBA_BLOCK, MOBA_W), B, S)
        h1, code, wk, cnt = _mix_ln_route(
            h, att_a, att_b, w_in_b[:, QKV_COLS:], w_branch_swa[l].astype(BF16), w_branch_moba[l].astype(BF16),
            w_out[l].astype(BF16), ln1_g[l].reshape(1, D), ln1_b[l].reshape(1, D),
            w_router[l].T, router_bias[l].reshape(N_EXPERTS, 1), alpha)
        h = _moe(h1, code, wk, cnt, w_exp_gate[l].astype(BF16), w_exp_up[l].astype(BF16),
                 w_exp_down[l].astype(BF16), w_sh_gate[l].astype(BF16), w_sh_up[l].astype(BF16),
                 w_sh_down[l].astype(BF16), ln2_g[l].reshape(1, D), ln2_b[l].reshape(1, D), alpha)
    return h.reshape(B, S, D)
```

```python
import functools

import numpy as np
import jax
import jax.numpy as jnp
from jax import lax
from jax.experimental import pallas as pl
from jax.experimental.pallas import tpu as pltpu

F32 = jnp.float32
BF16 = jnp.bfloat16

HEAD_DIM = 64
SWA_HEADS = 8
SWA_KV_HEADS = 2
SWA_BLOCK = 128
MOBA_HEADS = 8
MOBA_BLOCK = 256
MOBA_TOPK = 3
N_EXPERTS = 64
TOP_K = 8
N_GROUPS = 8
TOPK_GROUPS = 4
ROUTED_SCALE = 2.5
LN_EPS = 1e-5
NEG_INF = -1e30
REMOVED = -3e38

SWA_Q_W = SWA_HEADS * HEAD_DIM
SWA_KV_W = SWA_KV_HEADS * HEAD_DIM
MOBA_W = MOBA_HEADS * HEAD_DIM
QKV_COLS = SWA_Q_W + 2 * SWA_KV_W + 3 * MOBA_W

_N_SOFTMAX_HEADS = SWA_HEADS + MOBA_HEADS
_SLOPES = np.asarray(2.0 ** (-8.0 * np.arange(1, _N_SOFTMAX_HEADS + 1) / _N_SOFTMAX_HEADS), np.float32)
SWA_SLOPES = [float(s) for s in _SLOPES[:SWA_HEADS]]
MOBA_SLOPES = [float(s) for s in _SLOPES[SWA_HEADS:]]

LANES = 128
VMEM_LIMIT_BYTES = 56 * 1024 * 1024


def _dot(a, b):
    return jnp.dot(a, b, preferred_element_type=F32)


def _dot_nt(a, b, precision=None):
    return lax.dot_general(a, b, (((1,), (1,)), ((), ())), preferred_element_type=F32, precision=precision)


def _sigmoid(x):
    return 1.0 / (1.0 + jnp.exp(-x))


def _layernorm(z, g, b):
    mu = jnp.mean(z, axis=-1, keepdims=True)
    zc = z - mu
    var = jnp.mean(zc * zc, axis=-1, keepdims=True)
    return zc * lax.rsqrt(var + LN_EPS) * g + b


def _qkv_kernel(x_ref, w_ref, qa_ref, ka_ref, va_ref, qb_ref, kb_ref, vb_ref, km_ref, *, tm):
    xb = x_ref[...].astype(BF16)
    scale = HEAD_DIM ** -0.5

    def proj(lo, hi):
        return _dot(xb, w_ref[:, lo:hi])

    c = 0
    qa_ref[...] = (proj(c, c + SWA_Q_W) * scale).astype(BF16)
    c += SWA_Q_W
    ka_ref[...] = proj(c, c + SWA_KV_W).astype(BF16)
    c += SWA_KV_W
    va_ref[...] = proj(c, c + SWA_KV_W).astype(BF16)
    c += SWA_KV_W
    qb_ref[...] = (proj(c, c + MOBA_W) * scale).astype(BF16)
    c += MOBA_W
    kb = proj(c, c + MOBA_W)
    kb_ref[...] = kb.astype(BF16)
    for i in range(tm // MOBA_BLOCK):
        blk = kb[i * MOBA_BLOCK:(i + 1) * MOBA_BLOCK, :]
        km_ref[0, i:i + 1, :] = jnp.sum(blk, axis=0, keepdims=True) * (1.0 / MOBA_BLOCK)
    c += MOBA_W
    vb_ref[...] = proj(c, c + MOBA_W).astype(BF16)


def _qkv_proj(xf, w_qkv, tm=512):
    T, D = xf.shape
    nt = T // tm
    row = lambda w: pl.BlockSpec((tm, w), lambda i: (i, 0))
    outs = pl.pallas_call(
        functools.partial(_qkv_kernel, tm=tm),
        grid=(nt,),
        in_specs=[row(D), pl.BlockSpec((D, QKV_COLS), lambda i: (0, 0))],
        out_specs=[row(SWA_Q_W), row(SWA_KV_W), row(SWA_KV_W), row(MOBA_W), row(MOBA_W), row(MOBA_W),
                   pl.BlockSpec((1, tm // MOBA_BLOCK, MOBA_W), lambda i: (i, 0, 0))],
        out_shape=[jax.ShapeDtypeStruct((T, SWA_Q_W), BF16), jax.ShapeDtypeStruct((T, SWA_KV_W), BF16),
                   jax.ShapeDtypeStruct((T, SWA_KV_W), BF16), jax.ShapeDtypeStruct((T, MOBA_W), BF16),
                   jax.ShapeDtypeStruct((T, MOBA_W), BF16), jax.ShapeDtypeStruct((T, MOBA_W), BF16),
                   jax.ShapeDtypeStruct((nt, tm // MOBA_BLOCK, MOBA_W), F32)],
        compiler_params=pltpu.CompilerParams(dimension_semantics=("parallel",), vmem_limit_bytes=VMEM_LIMIT_BYTES),
        name="qkv_proj",
    )(xf, w_qkv)
    return outs


def _swa_bias_table():
    blk = SWA_BLOCK
    i = np.arange(blk)[:, None]
    j = np.arange(2 * blk)[None, :]
    dist = i + blk - j
    band = (dist >= 0) & (dist < blk)
    alibi = -(np.asarray(SWA_SLOPES, np.float32)[:, None, None] * dist.astype(np.float32)[None])
    tabs = [np.where((band & (j >= blk))[None], alibi, np.float32(NEG_INF)),
            np.where(band[None], alibi, np.float32(NEG_INF))]
    return jnp.asarray(np.stack(tabs).astype(np.float32))


def _swa_kernel(sink_ref, q_ref, kp_ref, ko_ref, vp_ref, vo_ref, bias_ref, o_ref):
    blk = SWA_BLOCK
    k = jnp.concatenate([kp_ref[...], ko_ref[...]], axis=0)
    v = jnp.concatenate([vp_ref[...], vo_ref[...]], axis=0)
    group = SWA_HEADS // SWA_KV_HEADS
    heads = range(SWA_HEADS)
    kvs = [h // group for h in heads]
    s = [_dot_nt(q_ref[:, h * HEAD_DIM:(h + 1) * HEAD_DIM], k[:, kvs[h] * HEAD_DIM:(kvs[h] + 1) * HEAD_DIM])
         + bias_ref[0, h] for h in heads]
    m = [jnp.maximum(jnp.max(jnp.maximum(s[h][:, :blk], s[h][:, blk:]), axis=-1, keepdims=True), sink_ref[h])
         for h in heads]
    e = [jnp.exp(s[h] - m[h]) for h in heads]
    denom = [jnp.sum(e[h][:, :blk] + e[h][:, blk:], axis=-1, keepdims=True) + jnp.exp(sink_ref[h] - m[h])
             for h in heads]
    outs = [_dot(e[h].astype(BF16), v[:, kvs[h] * HEAD_DIM:(kvs[h] + 1) * HEAD_DIM]) / denom[h] for h in heads]
    o_ref[...] = jnp.concatenate(outs, axis=-1).astype(BF16)


def _swa_attention(qa, ka, va, sinks, B, S):
    nq = S // SWA_BLOCK
    own = lambda b, n: (b * nq + n, 0)
    prev = lambda b, n: (b * nq + jnp.maximum(n - 1, 0), 0)
    kv_spec = lambda im: pl.BlockSpec((SWA_BLOCK, SWA_KV_W), im)
    return pl.pallas_call(
        _swa_kernel,
        grid=(B, nq),
        in_specs=[pl.BlockSpec(memory_space=pltpu.SMEM),
                  pl.BlockSpec((SWA_BLOCK, SWA_Q_W), own),
                  kv_spec(prev), kv_spec(own), kv_spec(prev), kv_spec(own),
                  pl.BlockSpec((1, SWA_HEADS, SWA_BLOCK, 2 * SWA_BLOCK), lambda b, n: (jnp.minimum(n, 1), 0, 0, 0))],
        out_specs=pl.BlockSpec((SWA_BLOCK, SWA_Q_W), own),
        out_shape=jax.ShapeDtypeStruct((B * S, SWA_Q_W), BF16),
        compiler_params=pltpu.CompilerParams(dimension_semantics=("parallel", "parallel")),
        name="swa_attention",
    )(sinks, qa, ka, ka, va, va, _swa_bias_table())


MOBA_ROWS = 256


def _moba_bias_table(S, nb):
    pos = np.arange(S)
    onehot = np.zeros((S, MOBA_HEADS, HEAD_DIM), np.float32)
    onehot[pos, :, pos // MOBA_BLOCK] = 1.0
    tab = jnp.asarray(onehot)
    rem = jnp.asarray(np.asarray(MOBA_SLOPES, np.float32)[None, :] * pos[:, None].astype(np.float32))
    for t in range(3):
        part = rem.astype(BF16).astype(F32)
        tab = tab.at[:, :, nb + t].set(part)
        rem = rem - part
    return tab.reshape(S, MOBA_HEADS * HEAD_DIM).astype(BF16)


def _moba_kernel(q_ref, k_ref, v_ref, km_ref, tab_ref, o_ref, kaug_sc, s_sc0, s_sc1, *, nb):
    s_scs = (s_sc0, s_sc1)
    qi = pl.program_id(2)
    blk = MOBA_BLOCK
    hd = HEAD_DIM
    rows_per = MOBA_ROWS

    @pl.when(qi == 0)
    def _():
        for hh in range(2):
            kaug_sc[:, 2 * hh * hd:(2 * hh + 1) * hd] = k_ref[:, hh * hd:(hh + 1) * hd]
            kaug_sc[:, (2 * hh + 1) * hd:(2 * hh + 2) * hd] = tab_ref[:, hh * hd:(hh + 1) * hd]

    ri = lax.broadcasted_iota(jnp.int32, (rows_per, blk), 0)
    cj = lax.broadcasted_iota(jnp.int32, (rows_per, blk), 1)
    n_iota = lax.broadcasted_iota(jnp.int32, (nb, blk), 0)
    lane64 = lax.broadcasted_iota(jnp.int32, (1, hd), 1)
    ones_row = jnp.where((lane64 >= nb) & (lane64 < nb + 3), 1.0, 0.0)
    out_lane = lax.broadcasted_iota(jnp.int32, (rows_per, 2 * hd), 1)

    def tile(c):
        q_augs = []
        for hh in range(2):
            lanes = slice(hh * hd, (hh + 1) * hd)
            qh = q_ref[:, lanes]
            g_t = _dot_nt(km_ref[0, :, lanes], qh.astype(F32), precision=lax.Precision.HIGHEST)
            rank = jnp.zeros((nb, blk), jnp.int32)
            for mblk in range(c):
                gm = g_t[mblk:mblk + 1, :]
                beats = (gm > g_t) | ((gm == g_t) & (mblk < n_iota))
                rank = rank + jnp.where(beats, 1, 0)
            keep = ((n_iota < c) & (rank < MOBA_TOPK)) | (n_iota == c)
            selb = jnp.where(keep, 0.0, NEG_INF)
            selb_t = jnp.concatenate([selb, jnp.zeros((128 - nb, blk), F32)], axis=0).T
            q_bias = (selb_t[:, :hd] + ones_row).astype(BF16)
            q_augs.append(jnp.concatenate([qh, q_bias], axis=1))
        chains = [(hh, rc) for rc in range(blk // rows_per) for hh in range(2)]
        mx = {}
        for hh, rc in chains:
            qa = q_augs[hh][rc * rows_per:(rc + 1) * rows_per]
            for n in range(c + 1):
                s = _dot_nt(qa, kaug_sc[n * blk:(n + 1) * blk, 2 * hh * hd:(2 * hh + 2) * hd])
                if n == c:
                    s = jnp.where(ri + rc * rows_per >= cj, s, NEG_INF)
                s_scs[hh][rc, n] = s
                t = jnp.maximum(s[:, :128], s[:, 128:])
                mx[hh, rc] = t if n == 0 else jnp.maximum(mx[hh, rc], t)
        m = {key: jnp.max(val, axis=-1, keepdims=True) for key, val in mx.items()}
        lsum, acc = {}, {}
        for hh, rc in chains:
            for n in range(c + 1):
                p = jnp.exp(s_scs[hh][rc, n] - m[hh, rc])
                t = p[:, :128] + p[:, 128:]
                pv = _dot(p.astype(BF16), v_ref[n * blk:(n + 1) * blk, :])
                lsum[hh, rc] = t if n == 0 else lsum[hh, rc] + t
                acc[hh, rc] = pv if n == 0 else acc[hh, rc] + pv
        for rc in range(blk // rows_per):
            o0, o1 = (acc[hh, rc] / jnp.sum(lsum[hh, rc], axis=-1, keepdims=True) for hh in range(2))
            o_ref[rc * rows_per:(rc + 1) * rows_per, :] = jnp.where(out_lane < hd, o0, o1).astype(BF16)

    for c in range(nb):
        pl.when(qi == c)(functools.partial(tile, c))


def _moba_attention(qb, kb, vb, kmean, B, S):
    nb = S // MOBA_BLOCK
    blk = MOBA_BLOCK
    hpairs = MOBA_HEADS // 2
    pair = 2 * HEAD_DIM
    tab = _moba_bias_table(S, nb)
    return pl.pallas_call(
        functools.partial(_moba_kernel, nb=nb),
        grid=(B, hpairs, nb),
        in_specs=[pl.BlockSpec((blk, pair), lambda b, h, q: (b * nb + q, h)),
                  pl.BlockSpec((S, pair), lambda b, h, q: (b, h)),
                  pl.BlockSpec((S, pair), lambda b, h, q: (b, h)),
                  pl.BlockSpec((1, nb, pair), lambda b, h, q: (b, 0, h)),
                  pl.BlockSpec((S, pair), lambda b, h, q: (0, h))],
        out_specs=pl.BlockSpec((blk, pair), lambda b, h, q: (b * nb + q, h)),
        out_shape=jax.ShapeDtypeStruct((B * S, MOBA_W), BF16),
        scratch_shapes=[pltpu.VMEM((S, 2 * pair), BF16),
                        pltpu.VMEM((blk // MOBA_ROWS, nb, MOBA_ROWS, blk), F32),
                        pltpu.VMEM((blk // MOBA_ROWS, nb, MOBA_ROWS, blk), F32)],
        compiler_params=pltpu.CompilerParams(dimension_semantics=("parallel", "parallel", "arbitrary"),
                                             vmem_limit_bytes=VMEM_LIMIT_BYTES),
        name="moba_attention",
    )(qb, kb, vb, kmean, tab)


def _route(scores_t, biased, tm):
    gsz = N_EXPERTS // N_GROUPS
    sub = lax.broadcasted_iota(jnp.int32, (gsz, tm), 0)
    grp = [biased[g * gsz:(g + 1) * gsz, :] for g in range(N_GROUPS)]
    gscore = []
    for v in grp:
        m1 = jnp.max(v, axis=0, keepdims=True)
        first = jnp.min(jnp.where(v == m1, sub, gsz), axis=0, keepdims=True)
        m2 = jnp.max(jnp.where(sub == first, REMOVED, v), axis=0, keepdims=True)
        gscore.append(m1 + m2)
    masked = []
    for g in range(N_GROUPS):
        rank = jnp.zeros((1, tm), jnp.int32)
        for g2 in range(N_GROUPS):
            if g2 == g:
                continue
            beats = (gscore[g2] >= gscore[g]) if g2 < g else (gscore[g2] > gscore[g])
            rank = rank + jnp.where(beats, 1, 0)
        masked.append(jnp.where(rank < TOPK_GROUPS, grp[g], NEG_INF))
    v = jnp.concatenate(masked, axis=0)
    e_iota = lax.broadcasted_iota(jnp.int32, (N_EXPERTS, tm), 0)
    picked = jnp.zeros((N_EXPERTS, tm), F32)
    experts, svals = [], []
    for _ in range(TOP_K):
        m = jnp.max(v, axis=0, keepdims=True)
        first = jnp.min(jnp.where(v == m, e_iota, N_EXPERTS), axis=0, keepdims=True)
        pick = e_iota == first
        experts.append(first)
        svals.append(jnp.sum(jnp.where(pick, scores_t, 0.0), axis=0, keepdims=True))
        picked = jnp.where(pick, 1.0, picked)
        v = jnp.where(pick, REMOVED, v)
    denom = svals[0]
    for s in svals[1:]:
        denom = denom + s
    weights = [s / denom * ROUTED_SCALE for s in svals]
    return experts, weights, picked


MOE_TILE = 2048
ROUTE_SHIFT = 12
assert MOE_TILE <= (1 << ROUTE_SHIFT)


def _mix_kernel(x_ref, aa_ref, ab_ref, wg_ref, wa_ref, wb_ref, wo_ref, g_ref, b_ref, wr_ref, rb_ref, tri_ref,
                h_ref, code_ref, wk_ref, cnt_ref, run_sc, *, alpha, tm):
    step = pl.program_id(0) % (MOE_TILE // tm)
    x = x_ref[...]
    xb = x.astype(BF16)
    d = x.shape[-1]
    ga = _dot(xb, wg_ref[:, :d])
    gb = _dot(xb, wg_ref[:, d:])
    y = _sigmoid(ga) * _dot(aa_ref[...], wa_ref[...]) + _sigmoid(gb) * _dot(ab_ref[...], wb_ref[...])
    mix = _dot(y.astype(BF16), wo_ref[...])
    h = _layernorm(alpha * x + mix, g_ref[...], b_ref[...])
    for j in range(d // LANES):
        h_ref[pl.ds(j, tm, stride=d // LANES), :] = h[:, j * LANES:(j + 1) * LANES]
    logits_t = _dot_nt(wr_ref[...], h, precision=lax.Precision.HIGHEST)
    scores_t = _sigmoid(logits_t)
    experts, weights, picked = _route(scores_t, scores_t + rb_ref[...], tm)

    @pl.when(step == 0)
    def _():
        run_sc[...] = jnp.zeros_like(run_sc)

    pos = _dot(picked.astype(BF16), tri_ref[...]) + run_sc[...]
    run_sc[...] += jnp.sum(picked, axis=1, keepdims=True)
    e_iota = lax.broadcasted_iota(jnp.int32, (N_EXPERTS, tm), 0)
    codes = []
    for k in range(TOP_K):
        rank = jnp.sum(jnp.where(e_iota == experts[k], pos, 0.0), axis=0, keepdims=True)
        codes.append(experts[k] * (1 << ROUTE_SHIFT) + rank.astype(jnp.int32))
    for c in range(tm // LANES):
        lanes = slice(c * LANES, (c + 1) * LANES)
        code_ref[0, c] = jnp.concatenate([v[:, lanes] for v in codes], axis=0)
        wk_ref[0, c] = jnp.concatenate([v[:, lanes] for v in weights], axis=0)

    picked_t = jnp.concatenate([picked, jnp.zeros((128 - N_EXPERTS, tm), F32)], axis=0).T
    counts = jnp.sum(picked_t, axis=0, keepdims=True).astype(jnp.int32)

    @pl.when(step == 0)
    def _():
        cnt_ref[0] = counts

    @pl.when(step != 0)
    def _():
        cnt_ref[0] += counts


def _mix_ln_route(xf, att_a, att_b, w_gates, w_a, w_b, w_o, ln_g, ln_b, w_router_t, router_bias, alpha, tm=512):
    T, D = xf.shape
    assert MOE_TILE % tm == 0 and T % MOE_TILE == 0
    steps = MOE_TILE // tm
    tri = jnp.asarray(np.triu(np.ones((tm, tm), np.float32), k=1), BF16)
    row = lambda w: pl.BlockSpec((tm, w), lambda i: (i, 0))
    col = pl.BlockSpec((1, tm // LANES, TOP_K, LANES), lambda i: (i // steps, i % steps, 0, 0))
    rec_shape = (T // MOE_TILE, MOE_TILE // LANES, TOP_K, LANES)
    full = lambda a: pl.BlockSpec(a.shape, lambda i: (0,) * a.ndim)
    return pl.pallas_call(
        functools.partial(_mix_kernel, alpha=alpha, tm=tm),
        grid=(T // tm,),
        in_specs=[row(D), row(SWA_Q_W), row(MOBA_W), full(w_gates), full(w_a), full(w_b), full(w_o),
                  full(ln_g), full(ln_b), full(w_router_t), full(router_bias), full(tri)],
        out_specs=[pl.BlockSpec((tm * (D // LANES), LANES), lambda i: (i, 0)), col, col, pl.BlockSpec((1, 1, 128), lambda i: (i // steps, 0, 0))],
        out_shape=[jax.ShapeDtypeStruct((T * (D // LANES), LANES), F32), jax.ShapeDtypeStruct(rec_shape, jnp.int32),
                   jax.ShapeDtypeStruct(rec_shape, F32), jax.ShapeDtypeStruct((T // MOE_TILE, 1, 128), jnp.int32)],
        scratch_shapes=[pltpu.VMEM((N_EXPERTS, 1), F32)],
        compiler_params=pltpu.CompilerParams(dimension_semantics=("arbitrary",), vmem_limit_bytes=VMEM_LIMIT_BYTES),
        name="mix_ln_route",
    )(xf, att_a, att_b, w_gates, w_a, w_b, w_o, ln_g, ln_b, w_router_t, router_bias, tri)


XY_STRIDE = MOE_TILE + 8
MOE_ROWS = 64
MOE_STATIC_BLOCKS = 12
MOE_PREFETCH_ROWS = 256
EDGE_ROWS = 256
GATHER_UNROLL = 16
SCATTER_UNROLL = 8


def _moe_kernel(h_ref, code_ref, wk_ref, cnt_ref, eg_ref, eu_ref, ed_ref, sg_ref, su_ref, sd_ref, g_ref, b_ref,
                o_ref, acc_sc, xy_a, xy_b, dst_sc, dst_s, wk_s, cnt_s, off_s, pick_s, sem, *, alpha, ch):
    tile = pl.program_id(0)
    pair = pl.program_id(1)
    tt = MOE_TILE

    def token_major_rows(ref, r0, rows):
        start = pl.multiple_of(r0 * ch, ch)
        return jnp.concatenate([ref[pl.ds(start + j, rows, stride=ch), :] for j in range(ch)], axis=1)

    def swiglu(xb, wg, wu, wd):
        a = _dot(xb, wg)
        return _dot((a * _sigmoid(a) * _dot(xb, wu)).astype(BF16), wd)

    n_picks = tt * TOP_K
    rec_rows = n_picks // LANES
    rows_per_expert = rec_rows // N_EXPERTS
    slot = tile % 2

    @pl.when((tile == 0) & (pair == 0))
    def _():
        xy_a[...] = jnp.zeros_like(xy_a)
        xy_b[...] = jnp.zeros_like(xy_b)
        for u in range(MOE_PREFETCH_ROWS):
            pick_s[2 * n_picks + u] = 0

    def prepare(j, slot_j):
        copies = [pltpu.make_async_copy(cnt_ref.at[j], cnt_s, sem.at[2]),
                  pltpu.make_async_copy(code_ref.at[j], dst_sc, sem.at[0])]
        for c in copies:
            c.start()
        for c in copies:
            c.wait()
        total = jnp.int32(0)
        for x in range(N_EXPERTS):
            off_s[slot_j * LANES + x] = total
            total = total + cnt_s[0, x]
        off_s[slot_j * LANES + N_EXPERTS] = total
        code = dst_sc[...]
        expert = code >> ROUTE_SHIFT
        dst = code & ((1 << ROUTE_SHIFT) - 1)
        for x in range(1, N_EXPERTS):
            dst = dst + jnp.where(expert == x, off_s[slot_j * LANES + x], 0)
        dst_sc[...] = dst + slot_j * n_picks
        dst_copy = pltpu.make_async_copy(dst_sc, dst_s, sem.at[0])
        dst_copy.start()
        dst_copy.wait()

    def build_rows(r0, rows):
        for rr in range(rows):
            r = r0 + rr
            for u in range(LANES):
                pick_s[dst_s[r, u]] = r * LANES + u

    @pl.when(pair == 0)
    def _():
        wk_copy = pltpu.make_async_copy(wk_ref.at[tile], wk_s, sem.at[1])
        wk_copy.start()

        @pl.when(tile == 0)
        def _():
            prepare(0, 0)
            lax.fori_loop(0, rec_rows, lambda r, c: (build_rows(r, 1), c)[1], 0)

        prepare(jnp.minimum(tile + 1, pl.num_programs(0) - 1), 1 - slot)
        wk_copy.wait()

        def shared_block(rb, carry):
            r0 = rb * EDGE_ROWS
            y = swiglu(token_major_rows(h_ref, r0, EDGE_ROWS).astype(BF16), sg_ref[...], su_ref[...], sd_ref[...])
            start = pl.multiple_of(r0 * ch, ch)
            for j in range(ch):
                acc_sc[pl.ds(start + j, EDGE_ROWS, stride=ch), :] = y[:, j * LANES:(j + 1) * LANES]
            return carry

        lax.fori_loop(0, tt // EDGE_ROWS, shared_block, 0)

    def expert_list(x):
        first = off_s[slot * LANES + x]
        return slot * n_picks + first, off_s[slot * LANES + x + 1] - first

    def token_rows(p):
        return pl.multiple_of(p & -8, ch)

    def gather(buf, first, m0, count):
        for u in range(count):
            m = m0 + u
            buf[pl.ds(m, ch, stride=XY_STRIDE), :] = h_ref[pl.ds(token_rows(pick_s[first + m]), ch), :]

    def gather_range(buf, first, start, stop):
        groups = (stop - start) // GATHER_UNROLL
        lax.fori_loop(0, groups, lambda g, c: (gather(buf, first, start + g * GATHER_UNROLL, GATHER_UNROLL), c)[1], 0)
        lax.fori_loop(start + groups * GATHER_UNROLL, stop, lambda m, c: (gather(buf, first, m, 1), c)[1], 0)

    def run_expert(e, w, cur, nxt):
        pbase, n = expert_list(e)
        pbase_next, n_next = expert_list(jnp.minimum(e + 1, N_EXPERTS - 1))

        def expert_rows(r0, rows):
            xb = jnp.concatenate([cur[pl.ds(j * XY_STRIDE + r0, rows), :] for j in range(ch)], axis=1).astype(BF16)
            y = swiglu(xb, eg_ref[w], eu_ref[w], ed_ref[w])
            for j in range(ch):
                cur[pl.ds(j * XY_STRIDE + r0, rows), :] = y[:, j * LANES:(j + 1) * LANES]

        n_blocks = (n + MOE_ROWS - 1) // MOE_ROWS

        def expert_static(v):
            build_rows(e * rows_per_expert, rows_per_expert)
            gather(nxt, pbase_next, 0, min(v * MOE_ROWS, MOE_PREFETCH_ROWS))
            if v > 0:
                expert_rows(0, v * MOE_ROWS)

        for v in range(MOE_STATIC_BLOCKS + 1):
            pl.when(n_blocks == v)(functools.partial(expert_static, v))

        @pl.when(n_blocks > MOE_STATIC_BLOCKS)
        def _():
            build_rows(e * rows_per_expert, rows_per_expert)

            def expert_block(rb, carry):
                expert_rows(pl.multiple_of(rb * MOE_ROWS, MOE_ROWS), MOE_ROWS)
                return carry

            lax.fori_loop(0, n_blocks, expert_block, 0)

        staged = jnp.where(n_blocks > MOE_STATIC_BLOCKS, 0, jnp.minimum(n_blocks * MOE_ROWS, MOE_PREFETCH_ROWS))
        gather_range(nxt, pbase_next, jnp.minimum(staged, n_next), n_next)

        def scatter(m0, count):
            new = []
            for u in range(count):
                m = m0 + u
                p = pick_s[pbase + m]
                t8 = token_rows(p)
                y = cur[pl.ds(m, ch, stride=XY_STRIDE), :]
                new.append((t8, acc_sc[pl.ds(t8, ch), :] + wk_s[p] * y))
            for t8, v in new:
                acc_sc[pl.ds(t8, ch), :] = v

        n_main = (n // SCATTER_UNROLL) * SCATTER_UNROLL
        lax.fori_loop(0, n // SCATTER_UNROLL, lambda g, c: (scatter(g * SCATTER_UNROLL, SCATTER_UNROLL), c)[1], 0)
        lax.fori_loop(n_main, n, lambda m, c: (scatter(m, 1), c)[1], 0)

    @pl.when(pair == 0)
    def _():
        first, count = expert_list(0)
        gather_range(xy_a, first, 0, count)

    run_expert(2 * pair, 0, xy_a, xy_b)
    run_expert(2 * pair + 1, 1, xy_b, xy_a)

    @pl.when(pair == pl.num_programs(1) - 1)
    def _():
        def final_block(rb, carry):
            r0 = rb * EDGE_ROWS
            z = alpha * token_major_rows(h_ref, r0, EDGE_ROWS) + token_major_rows(acc_sc, r0, EDGE_ROWS)
            o_ref[pl.ds(pl.multiple_of(r0, EDGE_ROWS), EDGE_ROWS), :] = _layernorm(z, g_ref[...], b_ref[...])
            return carry

        lax.fori_loop(0, tt // EDGE_ROWS, final_block, 0)


def _moe(h_tm, code, wk, cnt, eg, eu, ed, sg, su, sd, ln_g, ln_b, alpha):
    E, D, F = eg.shape
    ch = D // LANES
    T = h_tm.shape[0] // ch
    assert ch == 8 and TOP_K == 8, "a token fills one 8-sublane group; pick ids are token * 8 + round"
    tt = MOE_TILE
    nt = T // tt
    rec_rows = tt * TOP_K // LANES
    code = code.transpose(0, 1, 3, 2).reshape(nt, rec_rows, LANES)
    wk = wk.transpose(0, 1, 3, 2).reshape(nt, tt * TOP_K)
    full = lambda a: pl.BlockSpec(a.shape, lambda i, e: (0,) * a.ndim)
    return pl.pallas_call(
        functools.partial(_moe_kernel, alpha=alpha, ch=ch),
        grid=(T // tt, E // 2),
        in_specs=[pl.BlockSpec((tt * ch, LANES), lambda i, e: (i, 0), pipeline_mode=pl.Buffered(1)),
                  pl.BlockSpec(memory_space=pl.ANY),
                  pl.BlockSpec(memory_space=pl.ANY),
                  pl.BlockSpec(memory_space=pl.ANY),
                  pl.BlockSpec((2, D, F), lambda i, e: (e, 0, 0)),
                  pl.BlockSpec((2, D, F), lambda i, e: (e, 0, 0)),
                  pl.BlockSpec((2, F, D), lambda i, e: (e, 0, 0)),
                  full(sg), full(su), full(sd), full(ln_g), full(ln_b)],
        out_specs=pl.BlockSpec((tt, D), lambda i, e: (i, 0), pipeline_mode=pl.Buffered(1)),
        out_shape=jax.ShapeDtypeStruct((T, D), F32),
        scratch_shapes=[pltpu.VMEM((tt * ch, LANES), F32),
                        pltpu.VMEM((ch * XY_STRIDE, LANES), F32),
                        pltpu.VMEM((ch * XY_STRIDE, LANES), F32),
                        pltpu.VMEM((rec_rows, LANES), jnp.int32),
                        pltpu.SMEM((rec_rows, LANES), jnp.int32),
                        pltpu.SMEM((TOP_K * tt,), F32),
                        pltpu.SMEM((1, LANES), jnp.int32),
                        pltpu.SMEM((2 * LANES,), jnp.int32),
                        pltpu.SMEM((2 * TOP_K * tt + MOE_PREFETCH_ROWS,), jnp.int32),
                        pltpu.SemaphoreType.DMA((3,))],
        compiler_params=pltpu.CompilerParams(dimension_semantics=("arbitrary", "arbitrary"),
                                             vmem_limit_bytes=VMEM_LIMIT_BYTES),
        name="moe_experts",
    )(h_tm, code, wk, cnt, eg, eu, ed, sg, su, sd, ln_g, ln_b)


def kernel(x, w_in, swa_sinks, w_branch_swa, w_branch_moba, w_out, ln1_g, ln1_b, w_router, router_bias,
           w_exp_gate, w_exp_up, w_exp_down, w_sh_gate, w_sh_up, w_sh_down, ln2_g, ln2_b):
    B, S, D = x.shape
    depth = w_in.shape[0]
    alpha = (2.0 * depth) ** 0.25
    h = x.reshape(B * S, D)
    for l in range(depth):
        w_in_b = w_in[l].astype(BF16)
        qa, ka, va, qb, kb, vb, kmean = _qkv_proj(h, w_in_b[:, :QKV_COLS])
        att_a = _swa_attention(qa, ka, va, swa_sinks[l], B, S)
        att_b = _moba_attention(qb, kb, vb, kmean.reshape(B, S // MOBA_BLOCK, MOBA_W), B, S)
        h1, code, wk, cnt = _mix_ln_route(
            h, att_a, att_b, w_in_b[:, QKV_COLS:], w_branch_swa[l].astype(BF16), w_branch_moba[l].astype(BF16),
            w_out[l].astype(BF16), ln1_g[l].reshape(1, D), ln1_b[l].reshape(1, D),
            w_router[l].T, router_bias[l].reshape(N_EXPERTS, 1), alpha)
        h = _moe(h1, code, wk, cnt, w_exp_gate[l].astype(BF16), w_exp_up[l].astype(BF16),
                 w_exp_down[l].astype(BF16), w_sh_gate[l].astype(BF16), w_sh_up[l].astype(BF16),
                 w_sh_down[l].astype(BF16), ln2_g[l].reshape(1, D), ln2_b[l].reshape(1, D), alpha)
    return h.reshape(B, S, D)
```
